```python
import math
import jax
import jax.numpy as jnp
from jax import lax
import numpy as np

D_MODEL = 4096
BATCH = 4
SEQ = 4096
DEPTH = 2

GRID_W = 64
CTX_LEN = 256
MIX_WIDTH = D_MODEL
CONV_CH = MIX_WIDTH // 2
ATTN_WIDTH = MIX_WIDTH - CONV_CH
DIFF_HEAD_DIM = 64
DIFF_VALUE_DIM = 2 * DIFF_HEAD_DIM
N_DIFF_HEADS = ATTN_WIDTH // DIFF_VALUE_DIM
CONV_WIDTH = 31
CONV_PAD = CONV_WIDTH // 2
D_FF = 2 * D_MODEL
N_MOD = 9
ROPE_BASE = 10000.0
ROPE_AXIS_DIM = DIFF_HEAD_DIM // 2
ROPE_HALF = ROPE_AXIS_DIM // 2
Q_BLOCK = 128
LN_EPS = 1e-5
DEEPNORM_ALPHA = (2 * DEPTH) ** 0.25
DEEPNORM_BETA = (8 * DEPTH) ** -0.25
ATTN_SCALE = DIFF_HEAD_DIM ** -0.5
Q_OFF = 2 * CONV_CH
K_OFF = Q_OFF + ATTN_WIDTH
V_OFF = K_OFF + ATTN_WIDTH
IN_WIDTH = V_OFF + ATTN_WIDTH

kernel_name = "hybrid_conformer_diffattn_dit_trunk"


def _layer_norm(x, g=None, b=None):
    xf = x.astype(jnp.float32)
    mu = jnp.mean(xf, axis=-1, keepdims=True)
    var = jnp.mean(jnp.square(xf - mu), axis=-1, keepdims=True)
    y = (xf - mu) * lax.rsqrt(var + LN_EPS)
    if g is not None:
        y = y * g.astype(jnp.float32) + b.astype(jnp.float32)
    return y.astype(x.dtype)


def _post_norm(x, update, g, b):
    return _layer_norm(DEEPNORM_ALPHA * x + update, g, b)


def _modulate(h, shift, scale):
    return h * (1.0 + scale) + shift


def _swiglu(h, w_gate, w_up, w_down):
    return (jax.nn.silu(h @ w_gate) * (h @ w_up)) @ w_down


def _ffn_sublayer(x, shift, scale, gate, w_gate, w_up, w_down, g, b):
    y = _swiglu(_modulate(x, shift, scale), w_gate, w_up, w_down)
    return _post_norm(x, 0.5 * gate * y, g, b)


def _axial_rope_tables(rows):
    inv = 1.0 / (ROPE_BASE ** (jnp.arange(ROPE_HALF, dtype=jnp.float32) / ROPE_HALF))
    r = jnp.repeat(jnp.arange(rows, dtype=jnp.float32), GRID_W)
    col = jnp.tile(jnp.arange(GRID_W, dtype=jnp.float32), rows)
    ang_r = r[:, None] * inv[None, :]
    ang_c = col[:, None] * inv[None, :]
    ang = jnp.concatenate([ang_r, ang_r, ang_c, ang_c], axis=-1)
    return jnp.cos(ang), jnp.sin(ang)


def _apply_axial_rope(x, cos, sin):
    xs = x.reshape(*x.shape[:-1], 2, 2, ROPE_HALF)
    rot = jnp.stack([-xs[..., 1, :], xs[..., 0, :]], axis=-2).reshape(x.shape)
    cs = cos[None, :, None, None, :].astype(x.dtype)
    sn = sin[None, :, None, None, :].astype(x.dtype)
    return x * cs + rot * sn


def _qk_heads(p):
    return p.reshape(*p.shape[:-1], N_DIFF_HEADS, 2, DIFF_HEAD_DIM)


def _v_heads(p):
    return p.reshape(*p.shape[:-1], N_DIFF_HEADS, DIFF_VALUE_DIM)


def _diff_attend(q, k, v, lam):
    s = jnp.einsum('bqhmd,bkhmd->bhmqk', q.astype(jnp.float32), k.astype(jnp.float32)) * ATTN_SCALE
    p = jax.nn.softmax(s, axis=-1)
    a = p[:, :, 0] - lam * p[:, :, 1]
    o = jnp.einsum('bhqk,bkhe->bqhe', a, v.astype(jnp.float32))
    return o.astype(v.dtype)


def _diff_attend_blocked(q, k, v, lam):
    b, n = q.shape[:2]
    nb = n // Q_BLOCK
    qb = jnp.moveaxis(q.reshape(b, nb, Q_BLOCK, *q.shape[2:]), 1, 0)
    ob = lax.map(lambda qq: _diff_attend(qq, k, v, lam), qb)
    return jnp.moveaxis(ob, 0, 1).reshape(b, n, N_DIFF_HEADS, DIFF_VALUE_DIM)


def _diff_head_norm(o, g, lam_init):
    of = o.astype(jnp.float32)
    of = of * lax.rsqrt(jnp.mean(jnp.square(of), axis=-1, keepdims=True) + LN_EPS)
    of = of * g.astype(jnp.float32) * (1.0 - lam_init)
    return of.reshape(*o.shape[:-2], ATTN_WIDTH).astype(o.dtype)


def _conformer_conv(glu_in, dw, dw_b, g, b):
    a, gt = jnp.split(glu_in, 2, axis=-1)
    u = a * jax.nn.sigmoid(gt)
    y = lax.conv_general_dilated(
        u, dw[:, None, :].astype(u.dtype), window_strides=(1,),
        padding=[(CONV_PAD, CONV_PAD)], dimension_numbers=('NWC', 'WIO', 'NWC'),
        feature_group_count=u.shape[-1]) + dw_b
    return jax.nn.silu(_layer_norm(y, g, b))


def _mix_sublayer(x, conv_out, attn_out, gate, w_out, g, b):
    y = jnp.concatenate([conv_out, attn_out], axis=-1) @ w_out
    return _post_norm(x, gate * y, g, b)


def setup_inputs(seed: int = 0) -> dict:
    key = jax.random.key(seed)
    ks = jax.random.split(key, 26)
    D, F = D_MODEL, D_FF

    def nrm(k, shape, s):
        return s * jax.random.normal(k, shape, jnp.float32)

    return {
        'x': nrm(ks[0], (BATCH, SEQ, D), 1.0),
        'c': nrm(ks[1], (BATCH, D), 1.0),
        'ctx': nrm(ks[2], (BATCH, CTX_LEN, D), 1.0),
        'c_ctx': nrm(ks[3], (D,), 1.0),
        'w_mod': nrm(ks[4], (DEPTH, D, N_MOD * D), 0.5 * D ** -0.5),
        'b_mod': nrm(ks[5], (DEPTH, N_MOD * D), 0.01),
        'ln_g': 1.0 + nrm(ks[6], (DEPTH, 3, D), 0.01),
        'ln_b': nrm(ks[7], (DEPTH, 3, D), 0.01),
        'ffn1_w_gate': nrm(ks[8], (DEPTH, D, F), D ** -0.5),
        'ffn1_w_up': nrm(ks[9], (DEPTH, D, F), D ** -0.5),
        'ffn1_w_down': nrm(ks[10], (DEPTH, F, D), DEEPNORM_BETA * F ** -0.5),
        'w_in': nrm(ks[11], (DEPTH, D, IN_WIDTH), D ** -0.5),
        'b_glu': nrm(ks[12], (DEPTH, 2 * CONV_CH), 0.01),
        'conv_dw': nrm(ks[13], (DEPTH, CONV_WIDTH, CONV_CH), CONV_WIDTH ** -0.5),
        'conv_dw_b': nrm(ks[14], (DEPTH, CONV_CH), 0.01),
        'conv_ln_g': 1.0 + nrm(ks[15], (DEPTH, CONV_CH), 0.01),
        'conv_ln_b': nrm(ks[16], (DEPTH, CONV_CH), 0.01),
        'lambda_q1': nrm(ks[17], (DEPTH, DIFF_HEAD_DIM), 0.1),
        'lambda_k1': nrm(ks[18], (DEPTH, DIFF_HEAD_DIM), 0.1),
        'lambda_q2': nrm(ks[19], (DEPTH, DIFF_HEAD_DIM), 0.1),
        'lambda_k2': nrm(ks[20], (DEPTH, DIFF_HEAD_DIM), 0.1),
        'attn_subln_g': 1.0 + nrm(ks[21], (DEPTH, DIFF_VALUE_DIM), 0.01),
        'w_out': nrm(ks[22], (DEPTH, MIX_WIDTH, D), DEEPNORM_BETA * MIX_WIDTH ** -0.5),
        'ffn2_w_gate': nrm(ks[23], (DEPTH, D, F), D ** -0.5),
        'ffn2_w_up': nrm(ks[24], (DEPTH, D, F), D ** -0.5),
        'ffn2_w_down': nrm(ks[25], (DEPTH, F, D), DEEPNORM_BETA * F ** -0.5),
    }


def reference(x, c, ctx, c_ctx, w_mod, b_mod, ln_g, ln_b, ffn1_w_gate, ffn1_w_up, ffn1_w_down,
              w_in, b_glu, conv_dw, conv_dw_b, conv_ln_g, conv_ln_b, lambda_q1, lambda_k1,
              lambda_q2, lambda_k2, attn_subln_g, w_out, ffn2_w_gate, ffn2_w_up, ffn2_w_down):
    b, n = x.shape[0], x.shape[1]
    rows = n // GRID_W
    cos, sin = _axial_rope_tables(rows)

    xl = _layer_norm(x)
    xc = _layer_norm(ctx)

    for i in range(DEPTH):
        last = i == DEPTH - 1
        lam_init = 0.8 - 0.6 * math.exp(-0.3 * i)

        mod_l = (jax.nn.silu(c) @ w_mod[i] + b_mod[i])[:, None, :]
        mod_c = jax.nn.silu(c_ctx) @ w_mod[i] + b_mod[i]
        ml = jnp.split(mod_l, N_MOD, axis=-1)
        mc = jnp.split(mod_c, N_MOD, axis=-1)

        xl = _ffn_sublayer(xl, ml[0], ml[1], ml[2], ffn1_w_gate[i], ffn1_w_up[i], ffn1_w_down[i],
                           ln_g[i, 0], ln_b[i, 0])
        xc = _ffn_sublayer(xc, mc[0], mc[1], mc[2], ffn1_w_gate[i], ffn1_w_up[i], ffn1_w_down[i],
                           ln_g[i, 0], ln_b[i, 0])

        hl = _modulate(xl, ml[3], ml[4])
        hc = _modulate(xc, mc[3], mc[4])
        lam = (jnp.exp(jnp.sum(lambda_q1[i].astype(jnp.float32) * lambda_k1[i].astype(jnp.float32)))
               - jnp.exp(jnp.sum(lambda_q2[i].astype(jnp.float32) * lambda_k2[i].astype(jnp.float32)))
               + lam_init)

        pl = hl @ w_in[i]
        q_l = _apply_axial_rope(_qk_heads(pl[..., Q_OFF:K_OFF]), cos, sin)
        k_l = _apply_axial_rope(_qk_heads(pl[..., K_OFF:V_OFF]), cos, sin)
        v_l = _v_heads(pl[..., V_OFF:])

        if last:
            pc = hc @ w_in[i][:, K_OFF:]
            k_c = _qk_heads(pc[..., :ATTN_WIDTH])
            v_c = _v_heads(pc[..., ATTN_WIDTH:])
        else:
            pc = hc @ w_in[i]
            q_c = _qk_heads(pc[..., Q_OFF:K_OFF])
            k_c = _qk_heads(pc[..., K_OFF:V_OFF])
            v_c = _v_heads(pc[..., V_OFF:])

        k_all = jnp.concatenate([k_c, k_l], axis=1)
        v_all = jnp.concatenate([v_c, v_l], axis=1)
        attn_l = _diff_head_norm(_diff_attend_blocked(q_l, k_all, v_all, lam), attn_subln_g[i], lam_init)
        conv_l = _conformer_conv(pl[..., :Q_OFF] + b_glu[i], conv_dw[i], conv_dw_b[i],
                                 conv_ln_g[i], conv_ln_b[i])
        xl_new = _mix_sublayer(xl, conv_l, attn_l, ml[5], w_out[i], ln_g[i, 1], ln_b[i, 1])

        if not last:
            attn_c = _diff_head_norm(_diff_attend(q_c, k_c, v_c, lam), attn_subln_g[i], lam_init)
            conv_c = _conformer_conv(pc[..., :Q_OFF] + b_glu[i], conv_dw[i], conv_dw_b[i],
                                     conv_ln_g[i], conv_ln_b[i])
            xc = _mix_sublayer(xc, conv_c, attn_c, mc[5], w_out[i], ln_g[i, 1], ln_b[i, 1])
        xl = xl_new

        xl = _ffn_sublayer(xl, ml[6], ml[7], ml[8], ffn2_w_gate[i], ffn2_w_up[i], ffn2_w_down[i],
                           ln_g[i, 2], ln_b[i, 2])
        if not last:
            xc = _ffn_sublayer(xc, mc[6], mc[7], mc[8], ffn2_w_gate[i], ffn2_w_up[i], ffn2_w_down[i],
                               ln_g[i, 2], ln_b[i, 2])

    return xl
```

```python
import functools
import math

import jax
import jax.numpy as jnp
from jax import lax
from jax.experimental import pallas as pl
from jax.experimental.pallas import tpu as pltpu

GRID_W = 64
ROPE_BASE = 10000.0
LN_EPS = 1e-5
N_MOD = 9
LANES = 128
MOD_ROWS = 8
CONV_HALO = 16
VMEM_LIMIT_BYTES = 56 * 1024 * 1024

BF16 = jnp.bfloat16
F32 = jnp.float32


def _pick(n, pref):
    t = min(n, pref)
    while n % t:
        t //= 2
    return t


def _params(*sem):
    return pltpu.CompilerParams(dimension_semantics=sem, vmem_limit_bytes=VMEM_LIMIT_BYTES)


def _sigmoid(v):
    return 1.0 / (1.0 + jnp.exp(-v))


def _group_of(row0, seq, batch):
    return jnp.minimum(row0 // seq, batch)


def _mod_spec(d, tm, seq, batch, slot, order):
    def idx(*g):
        i = g[order]
        return (_group_of(i * tm, seq, batch) * N_MOD + slot, 0, 0)
    return pl.BlockSpec((1, 1, d), idx)


def _mod_kernel(c_ref, w_ref, b_ref, o_ref):
    c = c_ref[...]
    s = (c * _sigmoid(c)).astype(BF16)
    w = w_ref[0].astype(BF16)
    o_ref[0] = jnp.dot(s, w, preferred_element_type=F32) + b_ref[0]


def _mod_call(c_rows, w_mod, b_mod):
    depth, d, n = w_mod.shape
    tn = _pick(n, 512)
    return pl.pallas_call(
        _mod_kernel,
        grid=(depth, n // tn),
        in_specs=[pl.BlockSpec((MOD_ROWS, d), lambda l, j: (0, 0)),
                  pl.BlockSpec((1, d, tn), lambda l, j: (l, 0, j)),
                  pl.BlockSpec((1, 1, tn), lambda l, j: (l, 0, j))],
        out_specs=pl.BlockSpec((1, MOD_ROWS, tn), lambda l, j: (l, 0, j)),
        out_shape=jax.ShapeDtypeStruct((depth, MOD_ROWS, n), F32),
        compiler_params=_params("parallel", "parallel"),
        name="adaln_mod",
    )(c_rows, w_mod, b_mod.reshape(depth, 1, n))


def _layer_norm_rows(z):
    mu = jnp.mean(z, axis=-1, keepdims=True)
    zc = z - mu
    var = jnp.mean(zc * zc, axis=-1, keepdims=True)
    return zc * lax.rsqrt(var + LN_EPS)


def _entry_kernel(x_ref, sh_ref, sc_ref, xn_ref, xm_ref):
    y = _layer_norm_rows(x_ref[...])
    xn_ref[...] = y
    xm_ref[...] = (y * (1.0 + sc_ref[0]) + sh_ref[0]).astype(BF16)


def _entry_call(xs, mod, seq, batch, tm):
    t, d = xs.shape
    row = pl.BlockSpec((tm, d), lambda i: (i, 0))
    return pl.pallas_call(
        _entry_kernel,
        grid=(t // tm,),
        in_specs=[row, _mod_spec(d, tm, seq, batch, 0, 0), _mod_spec(d, tm, seq, batch, 1, 0)],
        out_specs=[row, row],
        out_shape=[jax.ShapeDtypeStruct((t, d), F32), jax.ShapeDtypeStruct((t, d), BF16)],
        compiler_params=_params("parallel"),
        name="entry_norm",
    )(xs, mod, mod)


def _postnorm_kernel(*refs, alpha, coef, with_next):
    if with_next:
        x_ref, y_ref, gate_ref, g_ref, b_ref, sh_ref, sc_ref, xn_ref, xm_ref = refs
    else:
        x_ref, y_ref, gate_ref, g_ref, b_ref, xn_ref = refs
    z = alpha * x_ref[...] + (coef * gate_ref[0]) * y_ref[...]
    y = _layer_norm_rows(z) * g_ref[...] + b_ref[...]
    xn_ref[...] = y
    if with_next:
        xm_ref[...] = (y * (1.0 + sc_ref[0]) + sh_ref[0]).astype(BF16)


def _postnorm_call(x, y, mod_gate, gate_slot, coef, g, b, mod_next, next_slots,
                   rows, seq, batch, alpha, tm):
    d = x.shape[1]
    row = pl.BlockSpec((tm, d), lambda i: (i, 0))
    vec = pl.BlockSpec((1, d), lambda i: (0, 0))
    with_next = mod_next is not None
    in_specs = [row, row, _mod_spec(d, tm, seq, batch, gate_slot, 0), vec, vec]
    args = [x, y, mod_gate, g.reshape(1, d), b.reshape(1, d)]
    out_specs = [row]
    out_shape = [jax.ShapeDtypeStruct((rows, d), F32)]
    if with_next:
        in_specs += [_mod_spec(d, tm, seq, batch, next_slots[0], 0),
                     _mod_spec(d, tm, seq, batch, next_slots[1], 0)]
        args += [mod_next, mod_next]
        out_specs.append(row)
        out_shape.append(jax.ShapeDtypeStruct((rows, d), BF16))
    out = pl.pallas_call(
        functools.partial(_postnorm_kernel, alpha=alpha, coef=coef, with_next=with_next),
        grid=(rows // tm,),
        in_specs=in_specs,
        out_specs=out_specs,
        out_shape=out_shape,
        compiler_params=_params("parallel"),
        name="post_norm",
    )(*args)
    return (out[0], out[1]) if with_next else (out[0], None)


def _dual_kernel(*refs, act, with_bias):
    if with_bias:
        a_ref, w1_ref, w2_ref, b1_ref, b2_ref, o_ref = refs
    else:
        a_ref, w1_ref, w2_ref, o_ref = refs
    a = a_ref[...]
    p1 = jnp.dot(a, w1_ref[...], preferred_element_type=F32)
    p2 = jnp.dot(a, w2_ref[...], preferred_element_type=F32)
    if with_bias:
        p1 = p1 + b1_ref[...]
        p2 = p2 + b2_ref[...]
    if act == "swiglu":
        r = (p1 * _sigmoid(p1)) * p2
    else:
        r = p1 * _sigmoid(p2)
    o_ref[...] = r.astype(o_ref.dtype)


def _dual_call(a, w1, off1, w2, off2, n_out, bias, act, out_dtype, rows, tm, tn):
    k = a.shape[1]
    with_bias = bias is not None
    in_specs = [pl.BlockSpec((tm, k), lambda j, i: (i, 0)),
                pl.BlockSpec((k, tn), lambda j, i: (0, j + off1 // tn)),
                pl.BlockSpec((k, tn), lambda j, i: (0, j + off2 // tn))]
    args = [a, w1, w2]
    if with_bias:
        in_specs += [pl.BlockSpec((1, tn), lambda j, i: (0, j + off1 // tn)),
                     pl.BlockSpec((1, tn), lambda j, i: (0, j + off2 // tn))]
        args += [bias, bias]
    return pl.pallas_call(
        functools.partial(_dual_kernel, act=act, with_bias=with_bias),
        grid=(n_out // tn, rows // tm),
        in_specs=in_specs,
        out_specs=pl.BlockSpec((tm, tn), lambda j, i: (i, j)),
        out_shape=jax.ShapeDtypeStruct((rows, n_out), out_dtype),
        compiler_params=_params("parallel", "parallel"),
        name="gated_proj_" + act,
    )(*args)


def _matmul_kernel(*refs, n_pairs):
    o_ref = refs[-1]
    acc = jnp.dot(refs[0][...], refs[n_pairs][...], preferred_element_type=F32)
    for p in range(1, n_pairs):
        acc = acc + jnp.dot(refs[p][...], refs[n_pairs + p][...], preferred_element_type=F32)
    o_ref[...] = acc.astype(o_ref.dtype)


def _matmul_call(a_list, w, n_out, out_dtype, rows, tm, tn):
    n_pairs = len(a_list)
    in_specs, w_specs, row_off = [], [], 0
    for a in a_list:
        kp = a.shape[1]
        in_specs.append(pl.BlockSpec((tm, kp), lambda j, i: (i, 0)))
        w_specs.append(pl.BlockSpec((kp, tn), lambda j, i, rb=row_off // kp: (rb, j)))
        assert row_off % kp == 0
        row_off += kp
    return pl.pallas_call(
        functools.partial(_matmul_kernel, n_pairs=n_pairs),
        grid=(n_out // tn, rows // tm),
        in_specs=in_specs + w_specs,
        out_specs=pl.BlockSpec((tm, tn), lambda j, i: (i, j)),
        out_shape=jax.ShapeDtypeStruct((rows, n_out), out_dtype),
        compiler_params=_params("parallel", "parallel"),
        name="proj",
    )(*a_list, *([w] * n_pairs))


def _rope_tables(seq, tm, d_head):
    axis_dim = d_head // 2
    half = axis_dim // 2
    inv = 1.0 / (ROPE_BASE ** (jnp.arange(half, dtype=F32) / half))
    pos = jnp.arange(seq, dtype=jnp.int32)
    r = (pos // GRID_W).astype(F32)
    col = (pos % GRID_W).astype(F32)
    ang_r = r[:, None] * inv[None, :]
    ang_c = col[:, None] * inv[None, :]
    ang = jnp.concatenate([ang_r, ang_r, ang_c, ang_c], axis=-1)
    ang = jnp.tile(ang, (1, LANES // d_head))
    first_half = (jnp.arange(LANES) % axis_dim) < half
    cos, sin = jnp.cos(ang), jnp.sin(ang)
    sin_up = jnp.where(first_half[None, :], -sin, 0.0)
    sin_dn = jnp.where(first_half[None, :], 0.0, sin)
    ident = jnp.stack([jnp.ones((tm, LANES), F32), jnp.zeros((tm, LANES), F32),
                       jnp.zeros((tm, LANES), F32)])
    return jnp.concatenate([jnp.stack([cos, sin_up, sin_dn]), ident], axis=1)


def _qkv_kernel(a_ref, w_ref, tab_ref, o_ref, *, n_q_tiles, n_rope_tiles, half, scale):
    j = pl.program_id(0)
    acc = jnp.dot(a_ref[...], w_ref[...], preferred_element_type=F32)
    tn = acc.shape[1]

    @pl.when(j < n_rope_tiles)
    def _():
        cos, sin_up, sin_dn = tab_ref[0], tab_ref[1], tab_ref[2]
        sc = jnp.where(j < n_q_tiles, scale, 1.0).astype(F32)
        for c in range(tn // LANES):
            v = acc[:, c * LANES:(c + 1) * LANES]
            up = pltpu.roll(v, LANES - half, 1)
            dn = pltpu.roll(v, half, 1)
            r = v * cos + up * sin_up + dn * sin_dn
            o_ref[:, c * LANES:(c + 1) * LANES] = (r * sc).astype(o_ref.dtype)

    @pl.when(j >= n_rope_tiles)
    def _():
        o_ref[...] = acc.astype(o_ref.dtype)


def _qkv_call(a, w, col_off, attn_w, tab, seq, t_lat, d_head, rows, tm, tn):
    k = a.shape[1]
    n_out = 3 * attn_w

    def tab_idx(j, i):
        r0 = i * tm
        return (0, jnp.where(r0 < t_lat, (r0 % seq) // tm, seq // tm), 0)

    return pl.pallas_call(
        functools.partial(_qkv_kernel, n_q_tiles=attn_w // tn, n_rope_tiles=2 * attn_w // tn,
                          half=d_head // 4, scale=d_head ** -0.5),
        grid=(n_out // tn, rows // tm),
        in_specs=[pl.BlockSpec((tm, k), lambda j, i: (i, 0)),
                  pl.BlockSpec((k, tn), lambda j, i: (0, j + col_off // tn)),
                  pl.BlockSpec((3, tm, LANES), tab_idx)],
        out_specs=pl.BlockSpec((tm, tn), lambda j, i: (i, j)),
        out_shape=jax.ShapeDtypeStruct((rows, n_out), BF16),
        compiler_params=_params("parallel", "parallel"),
        name="qkv_rope",
    )(a, w, tab)


_NT = (((1,), (1,)), ((), ()))


def _attn_kernel(lam_ref, g_ref, q_ref, kc_ref, vc_ref, kl_ref, vl_ref, o_ref, *,
                 lam_init, n_lat_tiles, d_head):
    qi = pl.program_id(2)
    tq = q_ref.shape[0]
    lv = lam_ref[...]
    lam = (jnp.exp(jnp.sum(lv[0:1] * lv[1:2], axis=-1, keepdims=True))
           - jnp.exp(jnp.sum(lv[2:3] * lv[3:4], axis=-1, keepdims=True)) + lam_init)

    q = q_ref[...]
    lane = lax.broadcasted_iota(jnp.int32, q.shape, 1)
    zero = jnp.zeros_like(q)
    qm = jnp.concatenate([jnp.where(lane < d_head, q, zero),
                          jnp.where(lane >= d_head, q, zero)], axis=0)

    def finish(o, l):
        o = o / l
        d = o[:tq] - lam * o[tq:]
        d = d * lax.rsqrt(jnp.mean(d * d, axis=-1, keepdims=True) + LN_EPS)
        o_ref[...] = (d * g_ref[...] * (1.0 - lam_init)).astype(o_ref.dtype)

    s_c = lax.dot_general(qm, kc_ref[...], _NT, preferred_element_type=F32)

    @pl.when(qi < n_lat_tiles)
    def _():
        s_l = lax.dot_general(qm, kl_ref[...], _NT, preferred_element_type=F32)
        m = jnp.maximum(jnp.max(s_c, axis=-1, keepdims=True), jnp.max(s_l, axis=-1, keepdims=True))
        p_c = jnp.exp(s_c - m)
        p_l = jnp.exp(s_l - m)
        l = jnp.sum(p_c, axis=-1, keepdims=True) + jnp.sum(p_l, axis=-1, keepdims=True)
        o = (jnp.dot(p_c.astype(BF16), vc_ref[...], preferred_element_type=F32)
             + jnp.dot(p_l.astype(BF16), vl_ref[...], preferred_element_type=F32))
        finish(o, l)

    @pl.when(qi >= n_lat_tiles)
    def _():
        m = jnp.max(s_c, axis=-1, keepdims=True)
        p_c = jnp.exp(s_c - m)
        l = jnp.sum(p_c, axis=-1, keepdims=True)
        finish(jnp.dot(p_c.astype(BF16), vc_ref[...], preferred_element_type=F32), l)


def _attn_call(qkv, lam_vecs, subln_g, lam_init, batch, seq, ctx_len, n_heads, d_head,
               with_ctx_queries, tq):
    dv = 2 * d_head
    assert dv == LANES
    t_lat = batch * seq
    n_lat = seq // tq
    n_ctx = ctx_len // tq if with_ctx_queries else 0
    rows = t_lat + (batch * ctx_len if with_ctx_queries else 0)

    def q_idx(b, h, qi):
        lat = b * n_lat + qi
        ctx = (t_lat + b * ctx_len) // tq + (qi - n_lat)
        return (jnp.where(qi < n_lat, lat, ctx), h)

    def out_idx(b, h, qi):
        return q_idx(b, h, qi)

    full = lambda shape: pl.BlockSpec(shape, lambda b, h, qi: (0, 0))
    return pl.pallas_call(
        functools.partial(_attn_kernel, lam_init=lam_init, n_lat_tiles=n_lat, d_head=d_head),
        grid=(batch, n_heads, n_lat + n_ctx),
        in_specs=[full((4, LANES)), full((1, dv)),
                  pl.BlockSpec((tq, dv), q_idx),
                  pl.BlockSpec((ctx_len, dv), lambda b, h, qi: (t_lat // ctx_len + b, n_heads + h)),
                  pl.BlockSpec((ctx_len, dv), lambda b, h, qi: (t_lat // ctx_len + b, 2 * n_heads + h)),
                  pl.BlockSpec((seq, dv), lambda b, h, qi: (b, n_heads + h)),
                  pl.BlockSpec((seq, dv), lambda b, h, qi: (b, 2 * n_heads + h))],
        out_specs=pl.BlockSpec((tq, dv), out_idx),
        out_shape=jax.ShapeDtypeStruct((rows, n_heads * dv), BF16),
        compiler_params=_params("parallel", "parallel", "arbitrary"),
        name="diff_attention",
    )(lam_vecs, subln_g.reshape(1, dv), qkv, qkv, qkv, qkv, qkv)


def _conv_kernel(prev_ref, cur_ref, next_ref, dw_ref, dwb_ref, g_ref, b_ref, o_ref,
                 xs_ref, y_ref, *, rows_tile, width, seq, ctx_len, t_lat, row_chunk):
    i = pl.program_id(0)
    r0 = i * rows_tile
    is_lat = r0 < t_lat
    per_seq = jnp.where(is_lat, seq // rows_tile, ctx_len // rows_tile)
    pos = jnp.where(is_lat, i % (seq // rows_tile), (i - t_lat // rows_tile) % (ctx_len // rows_tile))
    keep_prev = (pos > 0).astype(F32)
    keep_next = (pos < per_seq - 1).astype(F32)
    h = CONV_HALO
    xs_ref[0:h, :] = jnp.where(keep_prev > 0, prev_ref[...], 0.0)
    xs_ref[h:h + rows_tile, :] = cur_ref[...]
    xs_ref[h + rows_tile:2 * h + rows_tile, :] = jnp.where(keep_next > 0, next_ref[...], 0.0)

    base = h - width // 2
    n_ch = cur_ref.shape[1]

    def lane_chunk(c, carry):
        cols = pl.ds(pl.multiple_of(c * LANES, LANES), LANES)
        taps = [jnp.broadcast_to(dw_ref[t:t + 1, cols], (row_chunk, LANES)) for t in range(width)]
        bias = jnp.broadcast_to(dwb_ref[:, cols], (row_chunk, LANES))
        for row in range(0, rows_tile, row_chunk):
            acc = bias
            for t in range(width):
                acc = acc + xs_ref[row + base + t:row + base + t + row_chunk, cols] * taps[t]
            y_ref[row:row + row_chunk, cols] = acc
        return carry

    lax.fori_loop(0, n_ch // LANES, lane_chunk, 0)

    y = _layer_norm_rows(y_ref[...]) * g_ref[...] + b_ref[...]
    o_ref[...] = (y * _sigmoid(y)).astype(o_ref.dtype)


def _conv_call(u, dw, dw_b, ln_g, ln_b, rows, seq, ctx_len, t_lat, tile):
    width, n_ch = dw.shape
    assert width // 2 < CONV_HALO and tile % CONV_HALO == 0
    hb = tile // CONV_HALO
    n_halo_blocks = rows // CONV_HALO
    dw_pad = jnp.zeros((2 * CONV_HALO, n_ch), F32).at[:width].set(dw)
    vec = pl.BlockSpec((1, n_ch), lambda i: (0, 0))
    return pl.pallas_call(
        functools.partial(_conv_kernel, rows_tile=tile, width=width, seq=seq, ctx_len=ctx_len,
                          t_lat=t_lat, row_chunk=8),
        grid=(rows // tile,),
        in_specs=[pl.BlockSpec((CONV_HALO, n_ch), lambda i: (jnp.maximum(i * hb - 1, 0), 0)),
                  pl.BlockSpec((tile, n_ch), lambda i: (i, 0)),
                  pl.BlockSpec((CONV_HALO, n_ch),
                               lambda i: (jnp.minimum((i + 1) * hb, n_halo_blocks - 1), 0)),
                  pl.BlockSpec((2 * CONV_HALO, n_ch), lambda i: (0, 0)), vec, vec, vec],
        out_specs=pl.BlockSpec((tile, n_ch), lambda i: (i, 0)),
        out_shape=jax.ShapeDtypeStruct((rows, n_ch), BF16),
        scratch_shapes=[pltpu.VMEM((tile + 2 * CONV_HALO, n_ch), F32),
                        pltpu.VMEM((tile, n_ch), F32)],
        compiler_params=_params("parallel"),
        name="conformer_conv",
    )(u, u, u, dw_pad, dw_b.reshape(1, n_ch), ln_g.reshape(1, n_ch), ln_b.reshape(1, n_ch))


def kernel(x, c, ctx, c_ctx, w_mod, b_mod, ln_g, ln_b, ffn1_w_gate, ffn1_w_up, ffn1_w_down,
           w_in, b_glu, conv_dw, conv_dw_b, conv_ln_g, conv_ln_b, lambda_q1, lambda_k1,
           lambda_q2, lambda_k2, attn_subln_g, w_out, ffn2_w_gate, ffn2_w_up, ffn2_w_down):
    batch, seq, d = x.shape
    ctx_len = ctx.shape[1]
    depth = w_mod.shape[0]
    d_ff = ffn1_w_gate.shape[2]
    conv_ch = conv_dw.shape[2]
    d_head = lambda_q1.shape[1]
    dv = attn_subln_g.shape[1]
    attn_w = w_out.shape[1] - conv_ch
    n_heads = attn_w // dv
    t_lat, t_ctx = batch * seq, batch * ctx_len
    t_all = t_lat + t_ctx
    alpha = (2 * depth) ** 0.25
    assert batch + 1 <= MOD_ROWS and dv == 2 * d_head and t_lat % ctx_len == 0

    unit = math.gcd(seq, t_ctx)
    tm = _pick(unit, 1024)
    tm_ln = _pick(unit, 256)
    tq = _pick(math.gcd(seq, ctx_len), 256)
    conv_tile = _pick(math.gcd(seq, ctx_len), 256)

    c_rows = jnp.zeros((MOD_ROWS, d), F32).at[:batch].set(c).at[batch].set(c_ctx)
    mods = _mod_call(c_rows, w_mod, b_mod).reshape(depth, MOD_ROWS * N_MOD, 1, d)

    xs = jnp.concatenate([x.reshape(t_lat, d), ctx.reshape(t_ctx, d)], axis=0)
    xr, xm = _entry_call(xs, mods[0], seq, batch, tm_ln)

    tab = _rope_tables(seq, tm, d_head)
    bf = lambda w: w.astype(BF16)

    for i in range(depth):
        last = i == depth - 1
        lam_init = 0.8 - 0.6 * math.exp(-0.3 * i)
        mod = mods[i]
        norm = functools.partial(_postnorm_call, seq=seq, batch=batch, alpha=alpha, tm=tm_ln)

        hid = _dual_call(xm, bf(ffn1_w_gate[i]), 0, bf(ffn1_w_up[i]), 0, d_ff, None, "swiglu",
                         BF16, t_all, tm, _pick(d_ff, 512))
        y = _matmul_call([hid], bf(ffn1_w_down[i]), d, F32, t_all, _pick(tm, 512), _pick(d, 512))
        xr, xm = norm(xr, y, mod, 2, 0.5, ln_g[i, 0], ln_b[i, 0], mod, (3, 4), t_all)

        rows = t_lat if last else t_all
        w_in_b = bf(w_in[i])
        u = _dual_call(xm, w_in_b, 0, w_in_b, conv_ch, conv_ch, b_glu[i].reshape(1, -1), "glu",
                       F32, rows, tm, _pick(conv_ch, 512))
        qkv = _qkv_call(xm, w_in_b, 2 * conv_ch, attn_w, tab, seq, t_lat, d_head, t_all, tm,
                        _pick(attn_w, 1024))
        lam_vecs = jnp.zeros((4, LANES), F32).at[:, :d_head].set(
            jnp.stack([lambda_q1[i], lambda_k1[i], lambda_q2[i], lambda_k2[i]]).astype(F32))
        attn = _attn_call(qkv, lam_vecs, attn_subln_g[i], lam_init, batch, seq, ctx_len, n_heads,
                          d_head, not last, tq)
        conv = _conv_call(u, conv_dw[i], conv_dw_b[i], conv_ln_g[i], conv_ln_b[i], rows, seq,
                          ctx_len, t_lat, conv_tile)
        y = _matmul_call([conv, attn], bf(w_out[i]), d, F32, rows, tm, _pick(d, 1024))
        xr, xm = norm(xr, y, mod, 5, 1.0, ln_g[i, 1], ln_b[i, 1], mod, (6, 7), rows)

        hid = _dual_call(xm, bf(ffn2_w_gate[i]), 0, bf(ffn2_w_up[i]), 0, d_ff, None, "swiglu",
                         BF16, rows, tm, _pick(d_ff, 512))
        y = _matmul_call([hid], bf(ffn2_w_down[i]), d, F32, rows, _pick(tm, 512), _pick(d, 512))
        nxt = None if last else mods[i + 1]
        xr, xm = norm(xr, y, mod, 8, 0.5, ln_g[i, 2], ln_b[i, 2], nxt, (0, 1), rows)

    return xr.reshape(batch, seq, d)
```

```python
import functools
import math

import jax
import jax.numpy as jnp
from jax import lax
from jax.experimental import pallas as pl
from jax.experimental.pallas import tpu as pltpu

GRID_W = 64
ROPE_BASE = 10000.0
LN_EPS = 1e-5
N_MOD = 9
LANES = 128
MOD_ROWS = 8
ATTN_KEY_CHUNK = 512
LOG2_E = 1.4426950408889634
CONV_HALO = 16
VMEM_LIMIT_BYTES = 56 * 1024 * 1024

BF16 = jnp.bfloat16
F32 = jnp.float32


def _pick(n, pref):
    t = min(n, pref)
    while n % t:
        t //= 2
    return t


def _params(*sem):
    return pltpu.CompilerParams(dimension_semantics=sem, vmem_limit_bytes=VMEM_LIMIT_BYTES)


def _sigmoid(v):
    return 1.0 / (1.0 + jnp.exp(-v))


def _group_of(row0, seq, batch):
    return jnp.minimum(row0 // seq, batch)


def _mod_spec(d, tm, seq, batch, slot, order):
    def idx(*g):
        i = g[order]
        return (_group_of(i * tm, seq, batch) * N_MOD + slot, 0, 0)
    return pl.BlockSpec((1, 1, d), idx)


def _mod_kernel(c_ref, w_ref, b_ref, o_ref):
    c = c_ref[...]
    s = (c * _sigmoid(c)).astype(BF16)
    w = w_ref[0].astype(BF16)
    o_ref[0] = jnp.dot(s, w, preferred_element_type=F32) + b_ref[0]


def _mod_call(c_rows, w_mod, b_mod):
    depth, d, n = w_mod.shape
    tn = _pick(n, 512)
    return pl.pallas_call(
        _mod_kernel,
        grid=(depth, n // tn),
        in_specs=[pl.BlockSpec((MOD_ROWS, d), lambda l, j: (0, 0)),
                  pl.BlockSpec((1, d, tn), lambda l, j: (l, 0, j)),
                  pl.BlockSpec((1, 1, tn), lambda l, j: (l, 0, j))],
        out_specs=pl.BlockSpec((1, MOD_ROWS, tn), lambda l, j: (l, 0, j)),
        out_shape=jax.ShapeDtypeStruct((depth, MOD_ROWS, n), F32),
        compiler_params=_params("parallel", "parallel"),
        name="adaln_mod",
    )(c_rows, w_mod, b_mod.reshape(depth, 1, n))


def _layer_norm_rows(z):
    mu = jnp.mean(z, axis=-1, keepdims=True)
    zc = z - mu
    var = jnp.mean(zc * zc, axis=-1, keepdims=True)
    return zc * lax.rsqrt(var + LN_EPS)


def _entry_kernel(x_ref, sh_ref, sc_ref, xn_ref, xm_ref):
    y = _layer_norm_rows(x_ref[...])
    xn_ref[...] = y
    xm_ref[...] = (y * (1.0 + sc_ref[0]) + sh_ref[0]).astype(BF16)


def _entry_call(xs, mod, seq, batch, tm):
    t, d = xs.shape
    row = pl.BlockSpec((tm, d), lambda i: (i, 0))
    return pl.pallas_call(
        _entry_kernel,
        grid=(t // tm,),
        in_specs=[row, _mod_spec(d, tm, seq, batch, 0, 0), _mod_spec(d, tm, seq, batch, 1, 0)],
        out_specs=[row, row],
        out_shape=[jax.ShapeDtypeStruct((t, d), F32), jax.ShapeDtypeStruct((t, d), BF16)],
        compiler_params=_params("parallel"),
        name="entry_norm",
    )(xs, mod, mod)


def _postnorm_kernel(*refs, alpha, coef, with_next):
    if with_next:
        x_ref, y_ref, gate_ref, g_ref, b_ref, sh_ref, sc_ref, xn_ref, xm_ref = refs
    else:
        x_ref, y_ref, gate_ref, g_ref, b_ref, xn_ref = refs
    z = alpha * x_ref[...] + (coef * gate_ref[0]) * y_ref[...]
    y = _layer_norm_rows(z) * g_ref[...] + b_ref[...]
    xn_ref[...] = y
    if with_next:
        xm_ref[...] = (y * (1.0 + sc_ref[0]) + sh_ref[0]).astype(BF16)


def _postnorm_call(x, y, mod_gate, gate_slot, coef, g, b, mod_next, next_slots,
                   rows, seq, batch, alpha, tm):
    d = x.shape[1]
    row = pl.BlockSpec((tm, d), lambda i: (i, 0))
    vec = pl.BlockSpec((1, d), lambda i: (0, 0))
    with_next = mod_next is not None
    in_specs = [row, row, _mod_spec(d, tm, seq, batch, gate_slot, 0), vec, vec]
    args = [x, y, mod_gate, g.reshape(1, d), b.reshape(1, d)]
    out_specs = [row]
    out_shape = [jax.ShapeDtypeStruct((rows, d), F32)]
    if with_next:
        in_specs += [_mod_spec(d, tm, seq, batch, next_slots[0], 0),
                     _mod_spec(d, tm, seq, batch, next_slots[1], 0)]
        args += [mod_next, mod_next]
        out_specs.append(row)
        out_shape.append(jax.ShapeDtypeStruct((rows, d), BF16))
    out = pl.pallas_call(
        functools.partial(_postnorm_kernel, alpha=alpha, coef=coef, with_next=with_next),
        grid=(rows // tm,),
        in_specs=in_specs,
        out_specs=out_specs,
        out_shape=out_shape,
        compiler_params=_params("parallel"),
        name="post_norm",
    )(*args)
    return (out[0], out[1]) if with_next else (out[0], None)


def _dual_kernel(*refs, act, with_bias):
    if with_bias:
        a_ref, w1_ref, w2_ref, b1_ref, b2_ref, o_ref = refs
    else:
        a_ref, w1_ref, w2_ref, o_ref = refs
    a = a_ref[...]
    p1 = jnp.dot(a, w1_ref[...], preferred_element_type=F32)
    p2 = jnp.dot(a, w2_ref[...], preferred_element_type=F32)
    if with_bias:
        p1 = p1 + b1_ref[...]
        p2 = p2 + b2_ref[...]
    if act == "swiglu":
        r = (p1 * _sigmoid(p1)) * p2
    else:
        r = p1 * _sigmoid(p2)
    o_ref[...] = r.astype(o_ref.dtype)


def _dual_call(a, w1, off1, w2, off2, n_out, bias, act, out_dtype, rows, tm, tn):
    k = a.shape[1]
    with_bias = bias is not None
    in_specs = [pl.BlockSpec((tm, k), lambda j, i: (i, 0)),
                pl.BlockSpec((k, tn), lambda j, i: (0, j + off1 // tn)),
                pl.BlockSpec((k, tn), lambda j, i: (0, j + off2 // tn))]
    args = [a, w1, w2]
    if with_bias:
        in_specs += [pl.BlockSpec((1, tn), lambda j, i: (0, j + off1 // tn)),
                     pl.BlockSpec((1, tn), lambda j, i: (0, j + off2 // tn))]
        args += [bias, bias]
    return pl.pallas_call(
        functools.partial(_dual_kernel, act=act, with_bias=with_bias),
        grid=(n_out // tn, rows // tm),
        in_specs=in_specs,
        out_specs=pl.BlockSpec((tm, tn), lambda j, i: (i, j)),
        out_shape=jax.ShapeDtypeStruct((rows, n_out), out_dtype),
        compiler_params=_params("parallel", "parallel"),
        name="gated_proj_" + act,
    )(*args)


def _matmul_kernel(*refs, n_pairs):
    o_ref = refs[-1]
    acc = jnp.dot(refs[0][...], refs[n_pairs][...], preferred_element_type=F32)
    for p in range(1, n_pairs):
        acc = acc + jnp.dot(refs[p][...], refs[n_pairs + p][...], preferred_element_type=F32)
    o_ref[...] = acc.astype(o_ref.dtype)


def _matmul_call(a_list, w, n_out, out_dtype, rows, tm, tn):
    n_pairs = len(a_list)
    in_specs, w_specs, row_off = [], [], 0
    for a in a_list:
        kp = a.shape[1]
        in_specs.append(pl.BlockSpec((tm, kp), lambda j, i: (i, 0)))
        w_specs.append(pl.BlockSpec((kp, tn), lambda j, i, rb=row_off // kp: (rb, j)))
        assert row_off % kp == 0
        row_off += kp
    return pl.pallas_call(
        functools.partial(_matmul_kernel, n_pairs=n_pairs),
        grid=(n_out // tn, rows // tm),
        in_specs=in_specs + w_specs,
        out_specs=pl.BlockSpec((tm, tn), lambda j, i: (i, j)),
        out_shape=jax.ShapeDtypeStruct((rows, n_out), out_dtype),
        compiler_params=_params("parallel", "parallel"),
        name="proj",
    )(*a_list, *([w] * n_pairs))


def _rope_tables(seq, tm, d_head):
    axis_dim = d_head // 2
    half = axis_dim // 2
    inv = 1.0 / (ROPE_BASE ** (jnp.arange(half, dtype=F32) / half))
    pos = jnp.arange(seq, dtype=jnp.int32)
    r = (pos // GRID_W).astype(F32)
    col = (pos % GRID_W).astype(F32)
    ang_r = r[:, None] * inv[None, :]
    ang_c = col[:, None] * inv[None, :]
    ang = jnp.concatenate([ang_r, ang_r, ang_c, ang_c], axis=-1)
    ang = jnp.tile(ang, (1, LANES // d_head))
    first_half = (jnp.arange(LANES) % axis_dim) < half
    cos, sin = jnp.cos(ang), jnp.sin(ang)
    sin_up = jnp.where(first_half[None, :], -sin, 0.0)
    sin_dn = jnp.where(first_half[None, :], 0.0, sin)
    ident = jnp.stack([jnp.ones((tm, LANES), F32), jnp.zeros((tm, LANES), F32),
                       jnp.zeros((tm, LANES), F32)])
    return jnp.concatenate([jnp.stack([cos, sin_up, sin_dn]), ident], axis=1)


def _head_proj_kernel(*refs, rope, half, scale, transpose):
    if rope:
        a_ref, w_ref, tab_ref, o_ref = refs
        cos, sin_up, sin_dn = tab_ref[0], tab_ref[1], tab_ref[2]
    else:
        a_ref, w_ref, o_ref = refs
    acc = jnp.dot(a_ref[...], w_ref[...], preferred_element_type=F32)
    for c in range(acc.shape[1] // LANES):
        cols = slice(c * LANES, (c + 1) * LANES)
        r = acc[:, cols]
        if rope:
            up = pltpu.roll(r, LANES - half, 1)
            dn = pltpu.roll(r, half, 1)
            r = r * cos + up * sin_up + dn * sin_dn
        if scale != 1.0:
            r = r * scale
        if transpose:
            o_ref[cols, :] = r.T.astype(o_ref.dtype)
        else:
            o_ref[:, cols] = r.astype(o_ref.dtype)


def _head_proj_call(a, w, col_off, n_out, tab, seq, t_lat, d_head, scale, transpose, rows, tm, tn):
    k = a.shape[1]
    rope = tab is not None

    def tab_idx(j, i):
        r0 = i * tm
        return (0, jnp.where(r0 < t_lat, (r0 % seq) // tm, seq // tm), 0)

    in_specs = [pl.BlockSpec((tm, k), lambda j, i: (i, 0)),
                pl.BlockSpec((k, tn), lambda j, i: (0, j + col_off // tn))]
    args = [a, w]
    if rope:
        in_specs.append(pl.BlockSpec((3, tm, LANES), tab_idx))
        args.append(tab)
    if transpose:
        out_spec = pl.BlockSpec((tn, tm), lambda j, i: (j, i))
        out_shape = jax.ShapeDtypeStruct((n_out, rows), BF16)
    else:
        out_spec = pl.BlockSpec((tm, tn), lambda j, i: (i, j))
        out_shape = jax.ShapeDtypeStruct((rows, n_out), BF16)
    return pl.pallas_call(
        functools.partial(_head_proj_kernel, rope=rope, half=d_head // 4, scale=scale,
                          transpose=transpose),
        grid=(n_out // tn, rows // tm),
        in_specs=in_specs,
        out_specs=out_spec,
        out_shape=out_shape,
        compiler_params=_params("parallel", "parallel"),
        name="head_proj",
    )(*args)


NEG_BIG = -1e30


def _diff_attend(qt, segments, lam, gain, d_head, key_chunk):
    n = qt.shape[1]
    row = lax.broadcasted_iota(jnp.int32, qt.shape, 0)
    zero = jnp.zeros_like(qt)
    qmt = jnp.concatenate([jnp.where(row < d_head, qt, zero),
                           jnp.where(row >= d_head, qt, zero)], axis=1)
    m = jnp.full((1, 2 * n), NEG_BIG, F32)
    l = jnp.zeros((1, 2 * n), F32)
    acc = jnp.zeros((qt.shape[0], 2 * n), F32)
    for k_ref, vt_ref in segments:
        n_keys = k_ref.shape[0]
        for c0 in range(0, n_keys, key_chunk):
            keys = slice(c0, min(c0 + key_chunk, n_keys))
            s = jnp.dot(k_ref[keys, :], qmt, preferred_element_type=F32)
            m_new = jnp.maximum(m, jnp.max(s, axis=0, keepdims=True))
            alpha = jnp.exp2(m - m_new)
            p = jnp.exp2(s - m_new)
            l = l * alpha + jnp.sum(p, axis=0, keepdims=True)
            acc = acc * alpha + jnp.dot(vt_ref[:, keys], p.astype(BF16),
                                        preferred_element_type=F32)
            m = m_new
    o = acc * (1.0 / l)
    d = (o[:, :n] - lam * o[:, n:]).T
    return d * lax.rsqrt(jnp.mean(d * d, axis=-1, keepdims=True) + LN_EPS) * gain


def _attn_kernel(*refs, lam_init, d_head, key_chunk, with_ctx_queries):
    if with_ctx_queries:
        lam_ref, g_ref, qt_ref, kc_ref, vtc_ref, kl_ref, vtl_ref, qtc_ref, o_ref, oc_ref = refs
    else:
        lam_ref, g_ref, qt_ref, kc_ref, vtc_ref, kl_ref, vtl_ref, o_ref = refs
    lv = lam_ref[...]
    lam = (jnp.exp(jnp.sum(lv[0:1] * lv[1:2], axis=-1, keepdims=True))
           - jnp.exp(jnp.sum(lv[2:3] * lv[3:4], axis=-1, keepdims=True)) + lam_init)
    gain = g_ref[...] * (1.0 - lam_init)

    o = _diff_attend(qt_ref[...], [(kc_ref, vtc_ref), (kl_ref, vtl_ref)], lam, gain, d_head,
                     key_chunk)
    o_ref[...] = o.astype(o_ref.dtype)

    if with_ctx_queries:
        @pl.when(pl.program_id(2) == 0)
        def _():
            oc = _diff_attend(qtc_ref[...], [(kc_ref, vtc_ref)], lam, gain, d_head, key_chunk)
            oc_ref[...] = oc.astype(oc_ref.dtype)


def _attn_call(qt, kk, vt, lam_vecs, subln_g, lam_init, batch, seq, ctx_len, n_heads, d_head,
               with_ctx_queries, tq, key_chunk=ATTN_KEY_CHUNK):
    dv = 2 * d_head
    assert dv == LANES
    t_lat = batch * seq
    n_lat = seq // tq
    ctx_blk = t_lat // ctx_len

    full = lambda shape: pl.BlockSpec(shape, lambda b, h, qi: (0, 0))
    in_specs = [full((4, LANES)), full((1, dv)),
                pl.BlockSpec((dv, tq), lambda b, h, qi: (h, b * n_lat + qi)),
                pl.BlockSpec((ctx_len, dv), lambda b, h, qi: (ctx_blk + b, h)),
                pl.BlockSpec((dv, ctx_len), lambda b, h, qi: (h, ctx_blk + b)),
                pl.BlockSpec((seq, dv), lambda b, h, qi: (b, h)),
                pl.BlockSpec((dv, seq), lambda b, h, qi: (h, b))]
    args = [lam_vecs, subln_g.reshape(1, dv), qt, kk, vt, kk, vt]
    out_specs = [pl.BlockSpec((tq, dv), lambda b, h, qi: (b * n_lat + qi, h))]
    out_shape = [jax.ShapeDtypeStruct((t_lat, n_heads * dv), BF16)]
    if with_ctx_queries:
        in_specs.append(pl.BlockSpec((dv, ctx_len), lambda b, h, qi: (h, ctx_blk + b)))
        args.append(qt)
        out_specs.append(pl.BlockSpec((ctx_len, dv), lambda b, h, qi: (b, h)))
        out_shape.append(jax.ShapeDtypeStruct((batch * ctx_len, n_heads * dv), BF16))
    out = pl.pallas_call(
        functools.partial(_attn_kernel, lam_init=lam_init, d_head=d_head, key_chunk=key_chunk,
                          with_ctx_queries=with_ctx_queries),
        grid=(batch, n_heads, n_lat),
        in_specs=in_specs,
        out_specs=out_specs,
        out_shape=out_shape,
        compiler_params=_params("parallel", "parallel", "arbitrary"),
        name="diff_attention",
    )(*args)
    return (out[0], out[1]) if with_ctx_queries else (out[0], None)


def _conv_kernel(prev_ref, cur_ref, next_ref, dw_ref, dwb_ref, g_ref, b_ref, o_ref,
                 xs_ref, y_ref, *, rows_tile, width, seq, ctx_len, t_lat, row_chunk):
    i = pl.program_id(0)
    r0 = i * rows_tile
    is_lat = r0 < t_lat
    per_seq = jnp.where(is_lat, seq // rows_tile, ctx_len // rows_tile)
    pos = jnp.where(is_lat, i % (seq // rows_tile), (i - t_lat // rows_tile) % (ctx_len // rows_tile))
    keep_prev = (pos > 0).astype(F32)
    keep_next = (pos < per_seq - 1).astype(F32)
    h = CONV_HALO
    xs_ref[0:h, :] = jnp.where(keep_prev > 0, prev_ref[...], 0.0)
    xs_ref[h:h + rows_tile, :] = cur_ref[...]
    xs_ref[h + rows_tile:2 * h + rows_tile, :] = jnp.where(keep_next > 0, next_ref[...], 0.0)

    base = h - width // 2
    n_ch = cur_ref.shape[1]

    def lane_chunk(c, carry):
        cols = pl.ds(pl.multiple_of(c * LANES, LANES), LANES)
        taps = [jnp.broadcast_to(dw_ref[t:t + 1, cols], (row_chunk, LANES)) for t in range(width)]
        bias = jnp.broadcast_to(dwb_ref[:, cols], (row_chunk, LANES))
        for row in range(0, rows_tile, row_chunk):
            acc = bias
            for t in range(width):
                acc = acc + xs_ref[row + base + t:row + base + t + row_chunk, cols] * taps[t]
            y_ref[row:row + row_chunk, cols] = acc
        return carry

    lax.fori_loop(0, n_ch // LANES, lane_chunk, 0)

    y = _layer_norm_rows(y_ref[...]) * g_ref[...] + b_ref[...]
    o_ref[...] = (y * _sigmoid(y)).astype(o_ref.dtype)


def _conv_call(u, dw, dw_b, ln_g, ln_b, rows, seq, ctx_len, t_lat, tile):
    width, n_ch = dw.shape
    assert width // 2 < CONV_HALO and tile % CONV_HALO == 0
    hb = tile // CONV_HALO
    n_halo_blocks = rows // CONV_HALO
    dw_pad = jnp.zeros((2 * CONV_HALO, n_ch), F32).at[:width].set(dw)
    vec = pl.BlockSpec((1, n_ch), lambda i: (0, 0))
    return pl.pallas_call(
        functools.partial(_conv_kernel, rows_tile=tile, width=width, seq=seq, ctx_len=ctx_len,
                          t_lat=t_lat, row_chunk=8),
        grid=(rows // tile,),
        in_specs=[pl.BlockSpec((CONV_HALO, n_ch), lambda i: (jnp.maximum(i * hb - 1, 0), 0)),
                  pl.BlockSpec((tile, n_ch), lambda i: (i, 0)),
                  pl.BlockSpec((CONV_HALO, n_ch),
                               lambda i: (jnp.minimum((i + 1) * hb, n_halo_blocks - 1), 0)),
                  pl.BlockSpec((2 * CONV_HALO, n_ch), lambda i: (0, 0)), vec, vec, vec],
        out_specs=pl.BlockSpec((tile, n_ch), lambda i: (i, 0)),
        out_shape=jax.ShapeDtypeStruct((rows, n_ch), BF16),
        scratch_shapes=[pltpu.VMEM((tile + 2 * CONV_HALO, n_ch), F32),
                        pltpu.VMEM((tile, n_ch), F32)],
        compiler_params=_params("parallel"),
        name="conformer_conv",
    )(u, u, u, dw_pad, dw_b.reshape(1, n_ch), ln_g.reshape(1, n_ch), ln_b.reshape(1, n_ch))


def kernel(x, c, ctx, c_ctx, w_mod, b_mod, ln_g, ln_b, ffn1_w_gate, ffn1_w_up, ffn1_w_down,
           w_in, b_glu, conv_dw, conv_dw_b, conv_ln_g, conv_ln_b, lambda_q1, lambda_k1,
           lambda_q2, lambda_k2, attn_subln_g, w_out, ffn2_w_gate, ffn2_w_up, ffn2_w_down):
    batch, seq, d = x.shape
    ctx_len = ctx.shape[1]
    depth = w_mod.shape[0]
    d_ff = ffn1_w_gate.shape[2]
    conv_ch = conv_dw.shape[2]
    d_head = lambda_q1.shape[1]
    dv = attn_subln_g.shape[1]
    attn_w = w_out.shape[1] - conv_ch
    n_heads = attn_w // dv
    t_lat, t_ctx = batch * seq, batch * ctx_len
    t_all = t_lat + t_ctx
    alpha = (2 * depth) ** 0.25
    assert batch + 1 <= MOD_ROWS and dv == 2 * d_head and t_lat % ctx_len == 0

    unit = math.gcd(seq, t_ctx)
    tm = _pick(unit, 1024)
    tm_ln = _pick(unit, 256)
    tq = _pick(seq, 1024)
    conv_tile = _pick(math.gcd(seq, ctx_len), 256)

    c_rows = jnp.zeros((MOD_ROWS, d), F32).at[:batch].set(c).at[batch].set(c_ctx)
    mods = _mod_call(c_rows, w_mod, b_mod).reshape(depth, MOD_ROWS * N_MOD, 1, d)

    xs = jnp.concatenate([x.reshape(t_lat, d), ctx.reshape(t_ctx, d)], axis=0)
    xr, xm = _entry_call(xs, mods[0], seq, batch, tm_ln)

    tab = _rope_tables(seq, tm, d_head)
    bf = lambda w: w.astype(BF16)

    for i in range(depth):
        last = i == depth - 1
        lam_init = 0.8 - 0.6 * math.exp(-0.3 * i)
        mod = mods[i]
        norm = functools.partial(_postnorm_call, seq=seq, batch=batch, alpha=alpha, tm=tm_ln)

        hid = _dual_call(xm, bf(ffn1_w_gate[i]), 0, bf(ffn1_w_up[i]), 0, d_ff, None, "swiglu",
                         BF16, t_all, tm, _pick(d_ff, 512))
        y = _matmul_call([hid], bf(ffn1_w_down[i]), d, F32, t_all, _pick(tm, 512), _pick(d, 512))
        xr, xm = norm(xr, y, mod, 2, 0.5, ln_g[i, 0], ln_b[i, 0], mod, (3, 4), t_all)

        rows = t_lat if last else t_all
        w_in_b = bf(w_in[i])
        u = _dual_call(xm, w_in_b, 0, w_in_b, conv_ch, conv_ch, b_glu[i].reshape(1, -1), "glu",
                       F32, rows, tm, _pick(conv_ch, 512))
        head_proj = functools.partial(_head_proj_call, xm, w_in_b, n_out=attn_w, seq=seq,
                                      t_lat=t_lat, d_head=d_head, tm=tm, tn=_pick(attn_w, 1024))
        qt = head_proj(col_off=2 * conv_ch, tab=tab, scale=d_head ** -0.5 * LOG2_E,
                       transpose=True, rows=rows)
        kk = head_proj(col_off=2 * conv_ch + attn_w, tab=tab, scale=1.0, transpose=False,
                       rows=t_all)
        vt = head_proj(col_off=2 * conv_ch + 2 * attn_w, tab=None, scale=1.0, transpose=True,
                       rows=t_all)
        lam_vecs = jnp.zeros((4, LANES), F32).at[:, :d_head].set(
            jnp.stack([lambda_q1[i], lambda_k1[i], lambda_q2[i], lambda_k2[i]]).astype(F32))
        attn, attn_ctx = _attn_call(qt, kk, vt, lam_vecs, attn_subln_g[i], lam_init, batch, seq,
                                    ctx_len, n_heads, d_head, not last, tq)
        if not last:
            attn = jnp.concatenate([attn, attn_ctx], axis=0)
        conv = _conv_call(u, conv_dw[i], conv_dw_b[i], conv_ln_g[i], conv_ln_b[i], rows, seq,
                          ctx_len, t_lat, conv_tile)
        y = _matmul_call([conv, attn], bf(w_out[i]), d, F32, rows, tm, _pick(d, 1024))
        xr, xm = norm(xr, y, mod, 5, 1.0, ln_g[i, 1], ln_b[i, 1], mod, (6, 7), rows)

        hid = _dual_call(xm, bf(ffn2_w_gate[i]), 0, bf(ffn2_w_up[i]), 0, d_ff, None, "swiglu",
                         BF16, rows, tm, _pick(d_ff, 512))
        y = _matmul_call([hid], bf(ffn2_w_down[i]), d, F32, rows, _pick(tm, 512), _pick(d, 512))
        nxt = None if last else mods[i + 1]
        xr, xm = norm(xr, y, mod, 8, 0.5, ln_g[i, 2], ln_b[i, 2], nxt, (0, 1), rows)

    return xr.reshape(batch, seq, d)
```

```python
import functools
import math

import jax
import jax.numpy as jnp
from jax import lax
from jax.experimental import pallas as pl
from jax.experimental.pallas import tpu as pltpu

GRID_W = 64
ROPE_BASE = 10000.0
LN_EPS = 1e-5
N_MOD = 9
LANES = 128
SUBLANES = 8
MOD_ROWS = 8
ATTN_KEY_CHUNK = 512
LOG2_E = 1.4426950408889634
CONV_HALO = 16
VMEM_LIMIT_BYTES = 56 * 1024 * 1024

BF16 = jnp.bfloat16
F32 = jnp.float32


def _pick(n, pref):
    t = min(n, pref)
    while n % t:
        t //= 2
    return t


def _params(*sem):
    return pltpu.CompilerParams(dimension_semantics=sem, vmem_limit_bytes=VMEM_LIMIT_BYTES)


def _sigmoid(v):
    return 1.0 / (1.0 + jnp.exp(-v))


def _group_of(row0, seq, batch):
    return jnp.minimum(row0 // seq, batch)


def _mod_spec(d, tm, seq, batch, slot, order):
    def idx(*g):
        i = g[order]
        return (_group_of(i * tm, seq, batch) * N_MOD + slot, 0, 0)
    return pl.BlockSpec((1, 1, d), idx)


def _mod_kernel(c_ref, w_ref, b_ref, o_ref):
    c = c_ref[...]
    s = (c * _sigmoid(c)).astype(BF16)
    w = w_ref[0].astype(BF16)
    o_ref[0] = jnp.dot(s, w, preferred_element_type=F32) + b_ref[0]


def _mod_call(c_rows, w_mod, b_mod):
    depth, d, n = w_mod.shape
    tn = _pick(n, 512)
    return pl.pallas_call(
        _mod_kernel,
        grid=(depth, n // tn),
        in_specs=[pl.BlockSpec((MOD_ROWS, d), lambda l, j: (0, 0)),
                  pl.BlockSpec((1, d, tn), lambda l, j: (l, 0, j)),
                  pl.BlockSpec((1, 1, tn), lambda l, j: (l, 0, j))],
        out_specs=pl.BlockSpec((1, MOD_ROWS, tn), lambda l, j: (l, 0, j)),
        out_shape=jax.ShapeDtypeStruct((depth, MOD_ROWS, n), F32),
        compiler_params=_params("parallel", "parallel"),
        name="adaln_mod",
    )(c_rows, w_mod, b_mod.reshape(depth, 1, n))


def _layer_norm_rows(z):
    mu = jnp.mean(z, axis=-1, keepdims=True)
    zc = z - mu
    var = jnp.mean(zc * zc, axis=-1, keepdims=True)
    return zc * lax.rsqrt(var + LN_EPS)


def _entry_kernel(x_ref, c_ref, sh_ref, sc_ref, xn_ref, xm_ref, *, n_lat_tiles):
    def emit(src_ref):
        y = _layer_norm_rows(src_ref[...])
        xn_ref[...] = y
        xm_ref[...] = (y * (1.0 + sc_ref[0]) + sh_ref[0]).astype(BF16)

    i = pl.program_id(0)
    pl.when(i < n_lat_tiles)(lambda: emit(x_ref))
    pl.when(i >= n_lat_tiles)(lambda: emit(c_ref))


def _entry_call(x2, ctx2, mod, seq, batch, tm):
    (t_lat, d), t_ctx = x2.shape, ctx2.shape[0]
    n_lat = t_lat // tm
    t = t_lat + t_ctx
    row = pl.BlockSpec((tm, d), lambda i: (i, 0))
    return pl.pallas_call(
        functools.partial(_entry_kernel, n_lat_tiles=n_lat),
        grid=(t // tm,),
        in_specs=[pl.BlockSpec((tm, d), lambda i: (jnp.minimum(i, n_lat - 1), 0)),
                  pl.BlockSpec((tm, d), lambda i: (jnp.maximum(i - n_lat, 0), 0)),
                  _mod_spec(d, tm, seq, batch, 0, 0), _mod_spec(d, tm, seq, batch, 1, 0)],
        out_specs=[row, row],
        out_shape=[jax.ShapeDtypeStruct((t, d), F32), jax.ShapeDtypeStruct((t, d), BF16)],
        compiler_params=_params("parallel"),
        name="entry_norm",
    )(x2, ctx2, mod, mod)


def _postnorm_kernel(*refs, alpha, coef, with_next):
    if with_next:
        x_ref, y_ref, gate_ref, g_ref, b_ref, sh_ref, sc_ref, xn_ref, xm_ref = refs
    else:
        x_ref, y_ref, gate_ref, g_ref, b_ref, xn_ref = refs
    z = alpha * x_ref[...] + (coef * gate_ref[0]) * y_ref[...]
    y = _layer_norm_rows(z) * g_ref[...] + b_ref[...]
    xn_ref[...] = y
    if with_next:
        xm_ref[...] = (y * (1.0 + sc_ref[0]) + sh_ref[0]).astype(BF16)


def _postnorm_call(x, y, mod_gate, gate_slot, coef, g, b, mod_next, next_slots,
                   rows, seq, batch, alpha, tm):
    d = x.shape[1]
    row = pl.BlockSpec((tm, d), lambda i: (i, 0))
    vec = pl.BlockSpec((1, d), lambda i: (0, 0))
    with_next = mod_next is not None
    in_specs = [row, row, _mod_spec(d, tm, seq, batch, gate_slot, 0), vec, vec]
    args = [x, y, mod_gate, g.reshape(1, d), b.reshape(1, d)]
    out_specs = [row]
    out_shape = [jax.ShapeDtypeStruct((rows, d), F32)]
    if with_next:
        in_specs += [_mod_spec(d, tm, seq, batch, next_slots[0], 0),
                     _mod_spec(d, tm, seq, batch, next_slots[1], 0)]
        args += [mod_next, mod_next]
        out_specs.append(row)
        out_shape.append(jax.ShapeDtypeStruct((rows, d), BF16))
    out = pl.pallas_call(
        functools.partial(_postnorm_kernel, alpha=alpha, coef=coef, with_next=with_next),
        grid=(rows // tm,),
        in_specs=in_specs,
        out_specs=out_specs,
        out_shape=out_shape,
        compiler_params=_params("parallel"),
        name="post_norm",
    )(*args)
    return (out[0], out[1]) if with_next else (out[0], None)


def _weight_spec(k, tn, index_map):
    return pl.BlockSpec((None, k, tn), index_map, pipeline_mode=pl.Buffered(1))


def _cast_weights_once(w_refs, wb_refs):
    @pl.when(pl.program_id(1) == 0)
    def _():
        for w_ref, wb_ref in zip(w_refs, wb_refs):
            wb_ref[...] = w_ref[...].astype(BF16)


def _dual_kernel(*refs, act, with_bias):
    if with_bias:
        a_ref, w1_ref, w2_ref, b1_ref, b2_ref, o_ref, w1b_ref, w2b_ref = refs
    else:
        a_ref, w1_ref, w2_ref, o_ref, w1b_ref, w2b_ref = refs
    _cast_weights_once([w1_ref, w2_ref], [w1b_ref, w2b_ref])
    a = a_ref[...]
    p1 = jnp.dot(a, w1b_ref[...], preferred_element_type=F32)
    p2 = jnp.dot(a, w2b_ref[...], preferred_element_type=F32)
    if with_bias:
        p1 = p1 + b1_ref[...]
        p2 = p2 + b2_ref[...]
    if act == "swiglu":
        r = (p1 * _sigmoid(p1)) * p2
    else:
        r = p1 * _sigmoid(p2)
    o_ref[...] = r.astype(o_ref.dtype)


def _dual_call(a, w1, off1, w2, off2, layer, n_out, bias, act, out_dtype, rows, tm, tn):
    k = a.shape[1]
    with_bias = bias is not None
    in_specs = [pl.BlockSpec((tm, k), lambda j, i: (i, 0)),
                _weight_spec(k, tn, lambda j, i: (layer, 0, j + off1 // tn)),
                _weight_spec(k, tn, lambda j, i: (layer, 0, j + off2 // tn))]
    args = [a, w1, w2]
    if with_bias:
        in_specs += [pl.BlockSpec((1, tn), lambda j, i: (0, j + off1 // tn)),
                     pl.BlockSpec((1, tn), lambda j, i: (0, j + off2 // tn))]
        args += [bias, bias]
    return pl.pallas_call(
        functools.partial(_dual_kernel, act=act, with_bias=with_bias),
        grid=(n_out // tn, rows // tm),
        in_specs=in_specs,
        out_specs=pl.BlockSpec((tm, tn), lambda j, i: (i, j)),
        out_shape=jax.ShapeDtypeStruct((rows, n_out), out_dtype),
        scratch_shapes=[pltpu.VMEM((k, tn), BF16), pltpu.VMEM((k, tn), BF16)],
        compiler_params=_params("parallel", "arbitrary"),
        name="gated_proj_" + act,
    )(*args)


def _matmul_kernel(*refs, n_pairs):
    a_refs, w_refs = refs[:n_pairs], refs[n_pairs:2 * n_pairs]
    o_ref, wb_refs = refs[2 * n_pairs], refs[2 * n_pairs + 1:]
    _cast_weights_once(w_refs, wb_refs)
    acc = jnp.dot(a_refs[0][...], wb_refs[0][...], preferred_element_type=F32)
    for a_ref, wb_ref in zip(a_refs[1:], wb_refs[1:]):
        acc = acc + jnp.dot(a_ref[...], wb_ref[...], preferred_element_type=F32)
    o_ref[...] = acc.astype(o_ref.dtype)


def _matmul_call(a_list, w, layer, n_out, out_dtype, rows, tm, tn):
    n_pairs = len(a_list)
    in_specs, w_specs, scratch, row_off = [], [], [], 0
    for a in a_list:
        kp = a.shape[1]
        assert row_off % kp == 0
        in_specs.append(pl.BlockSpec((tm, kp), lambda j, i: (i, 0)))
        w_specs.append(_weight_spec(kp, tn, lambda j, i, rb=row_off // kp: (layer, rb, j)))
        scratch.append(pltpu.VMEM((kp, tn), BF16))
        row_off += kp
    return pl.pallas_call(
        functools.partial(_matmul_kernel, n_pairs=n_pairs),
        grid=(n_out // tn, rows // tm),
        in_specs=in_specs + w_specs,
        out_specs=pl.BlockSpec((tm, tn), lambda j, i: (i, j)),
        out_shape=jax.ShapeDtypeStruct((rows, n_out), out_dtype),
        scratch_shapes=scratch,
        compiler_params=_params("parallel", "arbitrary"),
        name="proj",
    )(*a_list, *([w] * n_pairs))


def _rope_tables(seq, tm, d_head):
    axis_dim = d_head // 2
    half = axis_dim // 2
    inv = 1.0 / (ROPE_BASE ** (jnp.arange(half, dtype=F32) / half))
    pos = jnp.arange(seq, dtype=jnp.int32)
    r = (pos // GRID_W).astype(F32)
    col = (pos % GRID_W).astype(F32)
    ang_r = r[:, None] * inv[None, :]
    ang_c = col[:, None] * inv[None, :]
    ang = jnp.concatenate([ang_r, ang_r, ang_c, ang_c], axis=-1)
    ang = jnp.tile(ang, (1, LANES // d_head))
    first_half = (jnp.arange(LANES) % axis_dim) < half
    cos, sin = jnp.cos(ang), jnp.sin(ang)
    sin_up = jnp.where(first_half[None, :], -sin, 0.0)
    sin_dn = jnp.where(first_half[None, :], 0.0, sin)
    ident = jnp.stack([jnp.ones((tm, LANES), F32), jnp.zeros((tm, LANES), F32),
                       jnp.zeros((tm, LANES), F32)])
    return jnp.concatenate([jnp.stack([cos, sin_up, sin_dn]), ident], axis=1)


def _head_proj_kernel(*refs, rope, half, scale, transpose):
    if rope:
        a_ref, w_ref, tab_ref, o_ref, wb_ref = refs
        cos, sin_up, sin_dn = tab_ref[0], tab_ref[1], tab_ref[2]
    else:
        a_ref, w_ref, o_ref, wb_ref = refs
    _cast_weights_once([w_ref], [wb_ref])
    acc = jnp.dot(a_ref[...], wb_ref[...], preferred_element_type=F32)
    for c in range(acc.shape[1] // LANES):
        cols = slice(c * LANES, (c + 1) * LANES)
        r = acc[:, cols]
        if rope:
            up = pltpu.roll(r, LANES - half, 1)
            dn = pltpu.roll(r, half, 1)
            r = r * cos + up * sin_up + dn * sin_dn
        if scale != 1.0:
            r = r * scale
        if transpose:
            o_ref[cols, :] = r.T.astype(o_ref.dtype)
        else:
            o_ref[:, cols] = r.astype(o_ref.dtype)


def _head_proj_call(a, w, layer, col_off, n_out, tab, seq, t_lat, d_head, scale, transpose, rows,
                    tm, tn):
    k = a.shape[1]
    rope = tab is not None

    def tab_idx(j, i):
        r0 = i * tm
        return (0, jnp.where(r0 < t_lat, (r0 % seq) // tm, seq // tm), 0)

    in_specs = [pl.BlockSpec((tm, k), lambda j, i: (i, 0)),
                _weight_spec(k, tn, lambda j, i: (layer, 0, j + col_off // tn))]
    args = [a, w]
    if rope:
        in_specs.append(pl.BlockSpec((3, tm, LANES), tab_idx))
        args.append(tab)
    if transpose:
        out_spec = pl.BlockSpec((tn, tm), lambda j, i: (j, i))
        out_shape = jax.ShapeDtypeStruct((n_out, rows), BF16)
    else:
        out_spec = pl.BlockSpec((tm, tn), lambda j, i: (i, j))
        out_shape = jax.ShapeDtypeStruct((rows, n_out), BF16)
    return pl.pallas_call(
        functools.partial(_head_proj_kernel, rope=rope, half=d_head // 4, scale=scale,
                          transpose=transpose),
        grid=(n_out // tn, rows // tm),
        in_specs=in_specs,
        out_specs=out_spec,
        out_shape=out_shape,
        scratch_shapes=[pltpu.VMEM((k, tn), BF16)],
        compiler_params=_params("parallel", "arbitrary"),
        name="head_proj",
    )(*args)


NEG_BIG = -1e30


def _diff_attend(qt, segments, lam, gain, d_head, key_chunk):
    n = qt.shape[1]
    row = lax.broadcasted_iota(jnp.int32, qt.shape, 0)
    zero = jnp.zeros_like(qt)
    qmt = jnp.concatenate([jnp.where(row < d_head, qt, zero),
                           jnp.where(row >= d_head, qt, zero)], axis=1)
    m = jnp.full((1, 2 * n), NEG_BIG, F32)
    l = jnp.zeros((1, 2 * n), F32)
    acc = jnp.zeros((qt.shape[0], 2 * n), F32)
    chunks = [(k_ref, vt_ref, slice(c0, min(c0 + key_chunk, k_ref.shape[0])))
              for k_ref, vt_ref in segments for c0 in range(0, k_ref.shape[0], key_chunk)]
    scores = lambda ch: jnp.dot(ch[0][ch[2], :], qmt, preferred_element_type=F32)
    s_next = scores(chunks[0])
    for i, (k_ref, vt_ref, keys) in enumerate(chunks):
        s, s_next = s_next, (scores(chunks[i + 1]) if i + 1 < len(chunks) else None)
        m_new = jnp.maximum(m, jnp.max(s, axis=0, keepdims=True))
        alpha = jnp.exp2(m - m_new)
        p = jnp.exp2(s - m_new)
        l = l * alpha + jnp.sum(p, axis=0, keepdims=True)
        acc = acc * alpha + jnp.dot(vt_ref[:, keys], p.astype(BF16),
                                    preferred_element_type=F32)
        m = m_new
    o = acc * (1.0 / l)
    d = (o[:, :n] - lam * o[:, n:]).T
    return d * lax.rsqrt(jnp.mean(d * d, axis=-1, keepdims=True) + LN_EPS) * gain


def _attn_kernel(*refs, lam_init, d_head, key_chunk, with_ctx_queries):
    if with_ctx_queries:
        lam_ref, g_ref, qt_ref, kc_ref, vtc_ref, kl_ref, vtl_ref, qtc_ref, o_ref, oc_ref = refs
    else:
        lam_ref, g_ref, qt_ref, kc_ref, vtc_ref, kl_ref, vtl_ref, o_ref = refs
    lv = lam_ref[...]
    lam = (jnp.exp(jnp.sum(lv[0:1] * lv[1:2], axis=-1, keepdims=True))
           - jnp.exp(jnp.sum(lv[2:3] * lv[3:4], axis=-1, keepdims=True)) + lam_init)
    gain = g_ref[...] * (1.0 - lam_init)

    o = _diff_attend(qt_ref[...], [(kc_ref, vtc_ref), (kl_ref, vtl_ref)], lam, gain, d_head,
                     key_chunk)
    o_ref[...] = o.astype(o_ref.dtype)

    if with_ctx_queries:
        @pl.when(pl.program_id(2) == 0)
        def _():
            oc = _diff_attend(qtc_ref[...], [(kc_ref, vtc_ref)], lam, gain, d_head, key_chunk)
            oc_ref[...] = oc.astype(oc_ref.dtype)


def _attn_call(qt, kk, vt, lam_vecs, subln_g, lam_init, batch, seq, ctx_len, n_heads, d_head,
               with_ctx_queries, tq, key_chunk=ATTN_KEY_CHUNK):
    dv = 2 * d_head
    assert dv == LANES
    t_lat = batch * seq
    n_lat = seq // tq
    ctx_blk = t_lat // ctx_len

    full = lambda shape: pl.BlockSpec(shape, lambda b, h, qi: (0, 0))
    in_specs = [full((4, LANES)), full((1, dv)),
                pl.BlockSpec((dv, tq), lambda b, h, qi: (h, b * n_lat + qi)),
                pl.BlockSpec((ctx_len, dv), lambda b, h, qi: (ctx_blk + b, h)),
                pl.BlockSpec((dv, ctx_len), lambda b, h, qi: (h, ctx_blk + b)),
                pl.BlockSpec((seq, dv), lambda b, h, qi: (b, h)),
                pl.BlockSpec((dv, seq), lambda b, h, qi: (h, b))]
    args = [lam_vecs, subln_g.reshape(1, dv), qt, kk, vt, kk, vt]
    out_specs = [pl.BlockSpec((tq, dv), lambda b, h, qi: (b * n_lat + qi, h))]
    out_shape = [jax.ShapeDtypeStruct((t_lat, n_heads * dv), BF16)]
    if with_ctx_queries:
        in_specs.append(pl.BlockSpec((dv, ctx_len), lambda b, h, qi: (h, ctx_blk + b)))
        args.append(qt)
        out_specs.append(pl.BlockSpec((ctx_len, dv), lambda b, h, qi: (b, h)))
        out_shape.append(jax.ShapeDtypeStruct((batch * ctx_len, n_heads * dv), BF16))
    out = pl.pallas_call(
        functools.partial(_attn_kernel, lam_init=lam_init, d_head=d_head, key_chunk=key_chunk,
                          with_ctx_queries=with_ctx_queries),
        grid=(batch, n_heads, n_lat),
        in_specs=in_specs,
        out_specs=out_specs,
        out_shape=out_shape,
        compiler_params=_params("parallel", "parallel", "arbitrary"),
        name="diff_attention",
    )(*args)
    return (out[0], out[1]) if with_ctx_queries else (out[0], None)


def _conv_kernel(prev_ref, cur_ref, next_ref, dw_ref, dwb_ref, g_ref, b_ref, o_ref,
                 xs_ref, sh_ref, y_ref, *, rows_tile, width, seq, ctx_len, t_lat):
    i = pl.program_id(0)
    r0 = i * rows_tile
    is_lat = r0 < t_lat
    per_seq = jnp.where(is_lat, seq // rows_tile, ctx_len // rows_tile)
    pos = jnp.where(is_lat, i % (seq // rows_tile), (i - t_lat // rows_tile) % (ctx_len // rows_tile))
    keep_prev = (pos > 0).astype(F32)
    keep_next = (pos < per_seq - 1).astype(F32)
    h = CONV_HALO
    xs_ref[0:h, :] = jnp.where(keep_prev > 0, prev_ref[...], 0.0)
    xs_ref[h:h + rows_tile, :] = cur_ref[...]
    xs_ref[h + rows_tile:2 * h + rows_tile, :] = jnp.where(keep_next > 0, next_ref[...], 0.0)

    base = h - width // 2
    n_ch = cur_ref.shape[1]
    sh_rows = sh_ref.shape[1]
    for s in range(1, SUBLANES):
        sh_ref[s - 1] = xs_ref[s:s + sh_rows, :]

    def lane_chunk(c, carry):
        cols = pl.ds(pl.multiple_of(c * LANES, LANES), LANES)
        taps = [jnp.broadcast_to(dw_ref[t:t + 1, cols], (SUBLANES, LANES)) for t in range(width)]
        bias = jnp.broadcast_to(dwb_ref[:, cols], (SUBLANES, LANES))
        for row in range(0, rows_tile, SUBLANES):
            acc = bias
            for t in range(width):
                s, r0 = (base + t) % SUBLANES, row + (base + t) // SUBLANES * SUBLANES
                src = xs_ref[r0:r0 + SUBLANES, cols] if s == 0 else sh_ref[s - 1, r0:r0 + SUBLANES, cols]
                acc = acc + src * taps[t]
            y_ref[row:row + SUBLANES, cols] = acc
        return carry

    lax.fori_loop(0, n_ch // LANES, lane_chunk, 0)

    y = _layer_norm_rows(y_ref[...]) * g_ref[...] + b_ref[...]
    o_ref[...] = (y * _sigmoid(y)).astype(o_ref.dtype)


def _conv_call(u, dw, dw_b, ln_g, ln_b, rows, seq, ctx_len, t_lat, tile):
    width, n_ch = dw.shape
    assert width // 2 < CONV_HALO and tile % CONV_HALO == 0
    hb = tile // CONV_HALO
    n_halo_blocks = rows // CONV_HALO
    dw_pad = jnp.zeros((2 * CONV_HALO, n_ch), F32).at[:width].set(dw)
    vec = pl.BlockSpec((1, n_ch), lambda i: (0, 0))
    sh_rows = tile + (CONV_HALO - width // 2 + width - 1) // SUBLANES * SUBLANES
    assert SUBLANES - 1 + sh_rows <= tile + 2 * CONV_HALO
    return pl.pallas_call(
        functools.partial(_conv_kernel, rows_tile=tile, width=width, seq=seq, ctx_len=ctx_len,
                          t_lat=t_lat),
        grid=(rows // tile,),
        in_specs=[pl.BlockSpec((CONV_HALO, n_ch), lambda i: (jnp.maximum(i * hb - 1, 0), 0)),
                  pl.BlockSpec((tile, n_ch), lambda i: (i, 0)),
                  pl.BlockSpec((CONV_HALO, n_ch),
                               lambda i: (jnp.minimum((i + 1) * hb, n_halo_blocks - 1), 0)),
                  pl.BlockSpec((2 * CONV_HALO, n_ch), lambda i: (0, 0)), vec, vec, vec],
        out_specs=pl.BlockSpec((tile, n_ch), lambda i: (i, 0)),
        out_shape=jax.ShapeDtypeStruct((rows, n_ch), BF16),
        scratch_shapes=[pltpu.VMEM((tile + 2 * CONV_HALO, n_ch), F32),
                        pltpu.VMEM((SUBLANES - 1, sh_rows, n_ch), F32),
                        pltpu.VMEM((tile, n_ch), F32)],
        compiler_params=_params("parallel"),
        name="conformer_conv",
    )(u, u, u, dw_pad, dw_b.reshape(1, n_ch), ln_g.reshape(1, n_ch), ln_b.reshape(1, n_ch))


def kernel(x, c, ctx, c_ctx, w_mod, b_mod, ln_g, ln_b, ffn1_w_gate, ffn1_w_up, ffn1_w_down,
           w_in, b_glu, conv_dw, conv_dw_b, conv_ln_g, conv_ln_b, lambda_q1, lambda_k1,
           lambda_q2, lambda_k2, attn_subln_g, w_out, ffn2_w_gate, ffn2_w_up, ffn2_w_down):
    batch, seq, d = x.shape
    ctx_len = ctx.shape[1]
    depth = w_mod.shape[0]
    d_ff = ffn1_w_gate.shape[2]
    conv_ch = conv_dw.shape[2]
    d_head = lambda_q1.shape[1]
    dv = attn_subln_g.shape[1]
    attn_w = w_out.shape[1] - conv_ch
    n_heads = attn_w // dv
    t_lat, t_ctx = batch * seq, batch * ctx_len
    t_all = t_lat + t_ctx
    alpha = (2 * depth) ** 0.25
    assert batch + 1 <= MOD_ROWS and dv == 2 * d_head and t_lat % ctx_len == 0

    unit = math.gcd(seq, t_ctx)
    tm = _pick(unit, 1024)
    tm_ln = _pick(unit, 256)
    tq = _pick(seq, 1024)
    conv_tile = _pick(math.gcd(seq, ctx_len), 256)

    c_rows = jnp.zeros((MOD_ROWS, d), F32).at[:batch].set(c).at[batch].set(c_ctx)
    mods = _mod_call(c_rows, w_mod, b_mod).reshape(depth, MOD_ROWS * N_MOD, 1, d)

    xr, xm = _entry_call(x.reshape(t_lat, d), ctx.reshape(t_ctx, d), mods[0], seq, batch, tm_ln)

    tab = _rope_tables(seq, tm, d_head)

    for i in range(depth):
        last = i == depth - 1
        lam_init = 0.8 - 0.6 * math.exp(-0.3 * i)
        mod = mods[i]
        norm = functools.partial(_postnorm_call, seq=seq, batch=batch, alpha=alpha, tm=tm_ln)

        hid = _dual_call(xm, ffn1_w_gate, 0, ffn1_w_up, 0, i, d_ff, None, "swiglu", BF16, t_all,
                         tm, _pick(d_ff, 512))
        y = _matmul_call([hid], ffn1_w_down, i, d, F32, t_all, _pick(tm, 512), _pick(d, 512))
        xr, xm = norm(xr, y, mod, 2, 0.5, ln_g[i, 0], ln_b[i, 0], mod, (3, 4), t_all)

        rows = t_lat if last else t_all
        u = _dual_call(xm, w_in, 0, w_in, conv_ch, i, conv_ch, b_glu[i].reshape(1, -1), "glu",
                       F32, rows, tm, _pick(conv_ch, 512))
        head_proj = functools.partial(_head_proj_call, xm, w_in, i, n_out=attn_w, seq=seq,
                                      t_lat=t_lat, d_head=d_head, tm=tm, tn=_pick(attn_w, 512))
        qt = head_proj(col_off=2 * conv_ch, tab=tab, scale=d_head ** -0.5 * LOG2_E,
                       transpose=True, rows=rows)
        kk = head_proj(col_off=2 * conv_ch + attn_w, tab=tab, scale=1.0, transpose=False,
                       rows=t_all)
        vt = head_proj(col_off=2 * conv_ch + 2 * attn_w, tab=None, scale=1.0, transpose=True,
                       rows=t_all)
        lam_vecs = jnp.zeros((4, LANES), F32).at[:, :d_head].set(
            jnp.stack([lambda_q1[i], lambda_k1[i], lambda_q2[i], lambda_k2[i]]).astype(F32))
        attn, attn_ctx = _attn_call(qt, kk, vt, lam_vecs, attn_subln_g[i], lam_init, batch, seq,
                                    ctx_len, n_heads, d_head, not last, tq)
        if not last:
            attn = jnp.concatenate([attn, attn_ctx], axis=0)
        conv = _conv_call(u, conv_dw[i], conv_dw_b[i], conv_ln_g[i], conv_ln_b[i], rows, seq,
                          ctx_len, t_lat, conv_tile)
        y = _matmul_call([conv, attn], w_out, i, d, F32, rows, tm, _pick(d, 1024))
        xr, xm = norm(xr, y, mod, 5, 1.0, ln_g[i, 1], ln_b[i, 1], mod, (6, 7), rows)

        hid = _dual_call(xm, ffn2_w_gate, 0, ffn2_w_up, 0, i, d_ff, None, "swiglu", BF16, rows,
                         tm, _pick(d_ff, 512))
        y = _matmul_call([hid], ffn2_w_down, i, d, F32, rows, _pick(tm, 512), _pick(d, 512))
        nxt = None if last else mods[i + 1]
        xr, xm = norm(xr, y, mod, 8, 0.5, ln_g[i, 2], ln_b[i, 2], nxt, (0, 1), rows)

    return xr.reshape(batch, seq, d)
```

```python
import functools
import math

import jax
import jax.numpy as jnp
from jax import lax
from jax.experimental import pallas as pl
from jax.experimental.pallas import tpu as pltpu

GRID_W = 64
ROPE_BASE = 10000.0
LN_EPS = 1e-5
N_MOD = 9
LANES = 128
SUBLANES = 8
BF16_ROWS = 16
MOD_ROWS = 8
ATTN_KEY_CHUNK = 512
ATTN_CHAINS = 4
LOG2_E = 1.4426950408889634
CONV_HALO = 16
VMEM_LIMIT_BYTES = 56 * 1024 * 1024

BF16 = jnp.bfloat16
F32 = jnp.float32


def _pick(n, pref):
    t = min(n, pref)
    while n % t:
        t //= 2
    return t


def _params(*sem):
    return pltpu.CompilerParams(dimension_semantics=sem, vmem_limit_bytes=VMEM_LIMIT_BYTES)


def _sigmoid(v):
    return 1.0 / (1.0 + jnp.exp(-v))


def _group_of(row0, seq, batch):
    return jnp.minimum(row0 // seq, batch)


def _mod_spec(d, tm, seq, batch, slot, order):
    def idx(*g):
        i = g[order]
        return (_group_of(i * tm, seq, batch) * N_MOD + slot, 0, 0)
    return pl.BlockSpec((1, 1, d), idx)


def _mod_kernel(c_ref, w_ref, b_ref, o_ref):
    c = c_ref[...]
    s = (c * _sigmoid(c)).astype(BF16)
    w = w_ref[0].astype(BF16)
    o_ref[0] = jnp.dot(s, w, preferred_element_type=F32) + b_ref[0]


def _mod_call(c_rows, w_mod, b_mod):
    depth, d, n = w_mod.shape
    tn = _pick(n, 512)
    return pl.pallas_call(
        _mod_kernel,
        grid=(depth, n // tn),
        in_specs=[pl.BlockSpec((MOD_ROWS, d), lambda l, j: (0, 0)),
                  pl.BlockSpec((1, d, tn), lambda l, j: (l, 0, j)),
                  pl.BlockSpec((1, 1, tn), lambda l, j: (l, 0, j))],
        out_specs=pl.BlockSpec((1, MOD_ROWS, tn), lambda l, j: (l, 0, j)),
        out_shape=jax.ShapeDtypeStruct((depth, MOD_ROWS, n), F32),
        compiler_params=_params("parallel", "parallel"),
        name="adaln_mod",
    )(c_rows, w_mod, b_mod.reshape(depth, 1, n))


def _layer_norm_rows(z):
    mu = jnp.mean(z, axis=-1, keepdims=True)
    zc = z - mu
    var = jnp.mean(zc * zc, axis=-1, keepdims=True)
    return zc * lax.rsqrt(var + LN_EPS)


def _entry_kernel(x_ref, c_ref, sh_ref, sc_ref, xn_ref, xm_ref, *, n_lat_tiles):
    def emit(src_ref):
        y = _layer_norm_rows(src_ref[...])
        xn_ref[...] = y
        xm_ref[...] = (y * (1.0 + sc_ref[0]) + sh_ref[0]).astype(BF16)

    i = pl.program_id(0)
    pl.when(i < n_lat_tiles)(lambda: emit(x_ref))
    pl.when(i >= n_lat_tiles)(lambda: emit(c_ref))


def _entry_call(x2, ctx2, mod, seq, batch, tm):
    (t_lat, d), t_ctx = x2.shape, ctx2.shape[0]
    n_lat = t_lat // tm
    t = t_lat + t_ctx
    row = pl.BlockSpec((tm, d), lambda i: (i, 0))
    return pl.pallas_call(
        functools.partial(_entry_kernel, n_lat_tiles=n_lat),
        grid=(t // tm,),
        in_specs=[pl.BlockSpec((tm, d), lambda i: (jnp.minimum(i, n_lat - 1), 0)),
                  pl.BlockSpec((tm, d), lambda i: (jnp.maximum(i - n_lat, 0), 0)),
                  _mod_spec(d, tm, seq, batch, 0, 0), _mod_spec(d, tm, seq, batch, 1, 0)],
        out_specs=[row, row],
        out_shape=[jax.ShapeDtypeStruct((t, d), F32), jax.ShapeDtypeStruct((t, d), BF16)],
        compiler_params=_params("parallel"),
        name="entry_norm",
    )(x2, ctx2, mod, mod)


def _postnorm_kernel(*refs, alpha, coef, with_next):
    if with_next:
        x_ref, y_ref, gate_ref, g_ref, b_ref, sh_ref, sc_ref, xn_ref, xm_ref = refs
    else:
        x_ref, y_ref, gate_ref, g_ref, b_ref, xn_ref = refs
    z = alpha * x_ref[...] + (coef * gate_ref[0]) * y_ref[...]
    y = _layer_norm_rows(z) * g_ref[...] + b_ref[...]
    xn_ref[...] = y
    if with_next:
        xm_ref[...] = (y * (1.0 + sc_ref[0]) + sh_ref[0]).astype(BF16)


def _postnorm_call(x, y, mod_gate, gate_slot, coef, g, b, mod_next, next_slots,
                   rows, seq, batch, alpha, tm):
    d = x.shape[1]
    row = pl.BlockSpec((tm, d), lambda i: (i, 0))
    vec = pl.BlockSpec((1, d), lambda i: (0, 0))
    with_next = mod_next is not None
    in_specs = [row, row, _mod_spec(d, tm, seq, batch, gate_slot, 0), vec, vec]
    args = [x, y, mod_gate, g.reshape(1, d), b.reshape(1, d)]
    out_specs = [row]
    out_shape = [jax.ShapeDtypeStruct((rows, d), F32)]
    if with_next:
        in_specs += [_mod_spec(d, tm, seq, batch, next_slots[0], 0),
                     _mod_spec(d, tm, seq, batch, next_slots[1], 0)]
        args += [mod_next, mod_next]
        out_specs.append(row)
        out_shape.append(jax.ShapeDtypeStruct((rows, d), BF16))
    out = pl.pallas_call(
        functools.partial(_postnorm_kernel, alpha=alpha, coef=coef, with_next=with_next),
        grid=(rows // tm,),
        in_specs=in_specs,
        out_specs=out_specs,
        out_shape=out_shape,
        compiler_params=_params("parallel"),
        name="post_norm",
    )(*args)
    return (out[0], out[1]) if with_next else (out[0], None)


def _stream_chunks(k, n_row_tiles):
    n = 1
    while n * 2 <= n_row_tiles and k % (n * 2 * BF16_ROWS) == 0:
        n *= 2
    return n


def _streamed_weight_spec(k, tn, n_chunks, n_sweeps, layer, row_off, col_off):
    kb = k // n_chunks
    assert row_off % kb == 0 and col_off % tn == 0

    def idx(s, i):
        return (layer, row_off // kb + jnp.minimum(i, n_chunks - 1),
                col_off // tn + jnp.minimum(s, n_sweeps - 1))
    return pl.BlockSpec((None, kb, tn), idx)


def _stage_weights(w_refs, wb_refs, n_chunks, n_sweeps):
    s, i = pl.program_id(0), pl.program_id(1)

    @pl.when(jnp.logical_and(i < n_chunks, s < n_sweeps))
    def _():
        for w_ref, wb_ref in zip(w_refs, wb_refs):
            kb = w_ref.shape[0]
            wb_ref[s % 2, pl.ds(pl.multiple_of(i * kb, kb), kb), :] = w_ref[...].astype(BF16)


def _row_tile(s, i):
    return jnp.where(s == 0, 0, i)


def _col_block(s):
    return jnp.maximum(s - 1, 0)


def _dual_kernel(*refs, act, with_bias, n_chunks, n_sweeps):
    if with_bias:
        a_ref, w1_ref, w2_ref, b1_ref, b2_ref, o_ref, w1b_ref, w2b_ref = refs
    else:
        a_ref, w1_ref, w2_ref, o_ref, w1b_ref, w2b_ref = refs
    _stage_weights([w1_ref, w2_ref], [w1b_ref, w2b_ref], n_chunks, n_sweeps)
    sweep = pl.program_id(0)
    slot = (sweep + 1) % 2

    @pl.when(sweep > 0)
    def _():
        a = a_ref[...]
        p1 = jnp.dot(a, w1b_ref[slot], preferred_element_type=F32)
        p2 = jnp.dot(a, w2b_ref[slot], preferred_element_type=F32)
        if with_bias:
            p1 = p1 + b1_ref[...]
            p2 = p2 + b2_ref[...]
        if act == "swiglu":
            r = (p1 * _sigmoid(p1)) * p2
        else:
            r = p1 * _sigmoid(p2)
        o_ref[...] = r.astype(o_ref.dtype)


def _dual_call(a, w1, off1, w2, off2, layer, n_out, bias, act, out_dtype, rows, tm, tn):
    k = a.shape[1]
    with_bias = bias is not None
    n_sweeps, n_tiles = n_out // tn, rows // tm
    n_chunks = _stream_chunks(k, n_tiles)
    in_specs = [pl.BlockSpec((tm, k), lambda s, i: (_row_tile(s, i), 0)),
                _streamed_weight_spec(k, tn, n_chunks, n_sweeps, layer, 0, off1),
                _streamed_weight_spec(k, tn, n_chunks, n_sweeps, layer, 0, off2)]
    args = [a, w1, w2]
    if with_bias:
        in_specs += [pl.BlockSpec((1, tn), lambda s, i: (0, _col_block(s) + off1 // tn)),
                     pl.BlockSpec((1, tn), lambda s, i: (0, _col_block(s) + off2 // tn))]
        args += [bias, bias]
    return pl.pallas_call(
        functools.partial(_dual_kernel, act=act, with_bias=with_bias, n_chunks=n_chunks,
                          n_sweeps=n_sweeps),
        grid=(n_sweeps + 1, n_tiles),
        in_specs=in_specs,
        out_specs=pl.BlockSpec((tm, tn), lambda s, i: (_row_tile(s, i), _col_block(s))),
        out_shape=jax.ShapeDtypeStruct((rows, n_out), out_dtype),
        scratch_shapes=[pltpu.VMEM((2, k, tn), BF16), pltpu.VMEM((2, k, tn), BF16)],
        compiler_params=_params("arbitrary", "arbitrary"),
        name="gated_proj_" + act,
    )(*args)


def _matmul_kernel(*refs, n_pairs, n_chunks, n_sweeps):
    a_refs, w_refs = refs[:n_pairs], refs[n_pairs:2 * n_pairs]
    o_ref, wb_refs = refs[2 * n_pairs], refs[2 * n_pairs + 1:]
    _stage_weights(w_refs, wb_refs, n_chunks, n_sweeps)
    sweep = pl.program_id(0)
    slot = (sweep + 1) % 2

    @pl.when(sweep > 0)
    def _():
        acc = jnp.dot(a_refs[0][...], wb_refs[0][slot], preferred_element_type=F32)
        for a_ref, wb_ref in zip(a_refs[1:], wb_refs[1:]):
            acc = acc + jnp.dot(a_ref[...], wb_ref[slot], preferred_element_type=F32)
        o_ref[...] = acc.astype(o_ref.dtype)


def _matmul_call(a_list, w, layer, n_out, out_dtype, rows, tm, tn):
    n_pairs = len(a_list)
    n_sweeps, n_tiles = n_out // tn, rows // tm
    n_chunks = min(_stream_chunks(a.shape[1], n_tiles) for a in a_list)
    in_specs, w_specs, scratch, row_off = [], [], [], 0
    for a in a_list:
        kp = a.shape[1]
        in_specs.append(pl.BlockSpec((tm, kp), lambda s, i: (_row_tile(s, i), 0)))
        w_specs.append(_streamed_weight_spec(kp, tn, n_chunks, n_sweeps, layer, row_off, 0))
        scratch.append(pltpu.VMEM((2, kp, tn), BF16))
        row_off += kp
    return pl.pallas_call(
        functools.partial(_matmul_kernel, n_pairs=n_pairs, n_chunks=n_chunks, n_sweeps=n_sweeps),
        grid=(n_sweeps + 1, n_tiles),
        in_specs=in_specs + w_specs,
        out_specs=pl.BlockSpec((tm, tn), lambda s, i: (_row_tile(s, i), _col_block(s))),
        out_shape=jax.ShapeDtypeStruct((rows, n_out), out_dtype),
        scratch_shapes=scratch,
        compiler_params=_params("arbitrary", "arbitrary"),
        name="proj",
    )(*a_list, *([w] * n_pairs))


def _rope_tables(seq, tm, d_head):
    axis_dim = d_head // 2
    half = axis_dim // 2
    inv = 1.0 / (ROPE_BASE ** (jnp.arange(half, dtype=F32) / half))
    pos = jnp.arange(seq, dtype=jnp.int32)
    r = (pos // GRID_W).astype(F32)
    col = (pos % GRID_W).astype(F32)
    ang_r = r[:, None] * inv[None, :]
    ang_c = col[:, None] * inv[None, :]
    ang = jnp.concatenate([ang_r, ang_r, ang_c, ang_c], axis=-1)
    ang = jnp.tile(ang, (1, LANES // d_head))
    first_half = (jnp.arange(LANES) % axis_dim) < half
    cos, sin = jnp.cos(ang), jnp.sin(ang)
    sin_up = jnp.where(first_half[None, :], -sin, 0.0)
    sin_dn = jnp.where(first_half[None, :], 0.0, sin)
    ident = jnp.stack([jnp.ones((tm, LANES), F32), jnp.zeros((tm, LANES), F32),
                       jnp.zeros((tm, LANES), F32)])
    return jnp.concatenate([jnp.stack([cos, sin_up, sin_dn]), ident], axis=1)


def _head_proj_kernel(*refs, rope, half, scale, transpose, n_chunks, n_sweeps):
    if rope:
        a_ref, w_ref, tab_ref, o_ref, wb_ref = refs
    else:
        a_ref, w_ref, o_ref, wb_ref = refs
    _stage_weights([w_ref], [wb_ref], n_chunks, n_sweeps)
    sweep = pl.program_id(0)
    slot = (sweep + 1) % 2

    @pl.when(sweep > 0)
    def _():
        acc = jnp.dot(a_ref[...], wb_ref[slot], preferred_element_type=F32)
        for c in range(acc.shape[1] // LANES):
            cols = slice(c * LANES, (c + 1) * LANES)
            r = acc[:, cols]
            if rope:
                up = pltpu.roll(r, LANES - half, 1)
                dn = pltpu.roll(r, half, 1)
                r = r * tab_ref[0] + up * tab_ref[1] + dn * tab_ref[2]
            if scale != 1.0:
                r = r * scale
            if transpose:
                o_ref[cols, :] = r.T.astype(o_ref.dtype)
            else:
                o_ref[:, cols] = r.astype(o_ref.dtype)


def _head_proj_call(a, w, layer, col_off, n_out, tab, seq, t_lat, d_head, scale, transpose, rows,
                    tm, tn):
    k = a.shape[1]
    rope = tab is not None
    n_sweeps, n_tiles = n_out // tn, rows // tm
    n_chunks = _stream_chunks(k, n_tiles)

    def tab_idx(s, i):
        r0 = _row_tile(s, i) * tm
        return (0, jnp.where(r0 < t_lat, (r0 % seq) // tm, seq // tm), 0)

    in_specs = [pl.BlockSpec((tm, k), lambda s, i: (_row_tile(s, i), 0)),
                _streamed_weight_spec(k, tn, n_chunks, n_sweeps, layer, 0, col_off)]
    args = [a, w]
    if rope:
        in_specs.append(pl.BlockSpec((3, tm, LANES), tab_idx))
        args.append(tab)
    if transpose:
        out_spec = pl.BlockSpec((tn, tm), lambda s, i: (_col_block(s), _row_tile(s, i)))
        out_shape = jax.ShapeDtypeStruct((n_out, rows), BF16)
    else:
        out_spec = pl.BlockSpec((tm, tn), lambda s, i: (_row_tile(s, i), _col_block(s)))
        out_shape = jax.ShapeDtypeStruct((rows, n_out), BF16)
    return pl.pallas_call(
        functools.partial(_head_proj_kernel, rope=rope, half=d_head // 4, scale=scale,
                          transpose=transpose, n_chunks=n_chunks, n_sweeps=n_sweeps),
        grid=(n_sweeps + 1, n_tiles),
        in_specs=in_specs,
        out_specs=out_spec,
        out_shape=out_shape,
        scratch_shapes=[pltpu.VMEM((2, k, tn), BF16)],
        compiler_params=_params("arbitrary", "arbitrary"),
        name="head_proj",
    )(*args)


NEG_BIG = -1e30


def _diff_attend(qts, segments, lam, gain, d_head, key_chunk):
    chunks = [(k_ref, vt_ref, slice(c0, min(c0 + key_chunk, k_ref.shape[0])))
              for k_ref, vt_ref in segments for c0 in range(0, k_ref.shape[0], key_chunk)]

    def scores(qmt, chunk):
        return jnp.dot(chunk[0][chunk[2], :], qmt, preferred_element_type=F32)

    state = []
    for qt in qts:
        n = qt.shape[1]
        row = lax.broadcasted_iota(jnp.int32, qt.shape, 0)
        zero = jnp.zeros_like(qt)
        qmt = jnp.concatenate([jnp.where(row < d_head, qt, zero),
                               jnp.where(row >= d_head, qt, zero)], axis=1)
        state.append(dict(qmt=qmt, m=jnp.full((1, 2 * n), NEG_BIG, F32),
                          l=jnp.zeros((1, 2 * n), F32),
                          acc=jnp.zeros((qt.shape[0], 2 * n), F32),
                          s_next=scores(qmt, chunks[0])))
    for i, (k_ref, vt_ref, keys) in enumerate(chunks):
        for st in state:
            s = st["s_next"]
            st["s_next"] = scores(st["qmt"], chunks[i + 1]) if i + 1 < len(chunks) else None
            m_new = jnp.maximum(st["m"], jnp.max(s, axis=0, keepdims=True))
            alpha = jnp.exp2(st["m"] - m_new)
            p = jnp.exp2(s - m_new)
            st["l"] = st["l"] * alpha + jnp.sum(p, axis=0, keepdims=True)
            st["acc"] = st["acc"] * alpha + jnp.dot(vt_ref[:, keys], p.astype(BF16),
                                                    preferred_element_type=F32)
            st["m"] = m_new
    outs = []
    for st in state:
        n = st["qmt"].shape[1] // 2
        o = st["acc"] * (1.0 / st["l"])
        d = (o[:, :n] - lam * o[:, n:]).T
        outs.append(d * lax.rsqrt(jnp.mean(d * d, axis=-1, keepdims=True) + LN_EPS) * gain)
    return outs


def _attn_kernel(*refs, lam_init, d_head, key_chunk, with_ctx_queries, chains):
    if with_ctx_queries:
        lam_ref, g_ref, qt_ref, kc_ref, vtc_ref, kl_ref, vtl_ref, qtc_ref, o_ref, oc_ref = refs
    else:
        lam_ref, g_ref, qt_ref, kc_ref, vtc_ref, kl_ref, vtl_ref, o_ref = refs
    lv = lam_ref[...]
    lam = (jnp.exp(jnp.sum(lv[0:1] * lv[1:2], axis=-1, keepdims=True))
           - jnp.exp(jnp.sum(lv[2:3] * lv[3:4], axis=-1, keepdims=True)) + lam_init)
    gain = g_ref[...] * (1.0 - lam_init)

    n = qt_ref.shape[1] // chains
    groups = [slice(g * n, (g + 1) * n) for g in range(chains)]
    outs = _diff_attend([qt_ref[:, g] for g in groups], [(kc_ref, vtc_ref), (kl_ref, vtl_ref)],
                        lam, gain, d_head, key_chunk)
    for g, o in zip(groups, outs):
        o_ref[g, :] = o.astype(o_ref.dtype)

    if with_ctx_queries:
        @pl.when(pl.program_id(2) == 0)
        def _():
            oc, = _diff_attend([qtc_ref[...]], [(kc_ref, vtc_ref)], lam, gain, d_head, key_chunk)
            oc_ref[...] = oc.astype(oc_ref.dtype)


def _attn_call(qt, kk, vt, lam_vecs, subln_g, lam_init, batch, seq, ctx_len, n_heads, d_head,
               with_ctx_queries, tq, key_chunk=ATTN_KEY_CHUNK, chains=ATTN_CHAINS):
    dv = 2 * d_head
    assert dv == LANES
    t_lat = batch * seq
    n_lat = seq // tq
    ctx_blk = t_lat // ctx_len

    full = lambda shape: pl.BlockSpec(shape, lambda b, h, qi: (0, 0))
    in_specs = [full((4, LANES)), full((1, dv)),
                pl.BlockSpec((dv, tq), lambda b, h, qi: (h, b * n_lat + qi)),
                pl.BlockSpec((ctx_len, dv), lambda b, h, qi: (ctx_blk + b, h)),
                pl.BlockSpec((dv, ctx_len), lambda b, h, qi: (h, ctx_blk + b)),
                pl.BlockSpec((seq, dv), lambda b, h, qi: (b, h)),
                pl.BlockSpec((dv, seq), lambda b, h, qi: (h, b))]
    args = [lam_vecs, subln_g.reshape(1, dv), qt, kk, vt, kk, vt]
    out_specs = [pl.BlockSpec((tq, dv), lambda b, h, qi: (b * n_lat + qi, h))]
    out_shape = [jax.ShapeDtypeStruct((t_lat, n_heads * dv), BF16)]
    if with_ctx_queries:
        in_specs.append(pl.BlockSpec((dv, ctx_len), lambda b, h, qi: (h, ctx_blk + b)))
        args.append(qt)
        out_specs.append(pl.BlockSpec((ctx_len, dv), lambda b, h, qi: (b, h)))
        out_shape.append(jax.ShapeDtypeStruct((batch * ctx_len, n_heads * dv), BF16))
    out = pl.pallas_call(
        functools.partial(_attn_kernel, lam_init=lam_init, d_head=d_head, key_chunk=key_chunk,
                          with_ctx_queries=with_ctx_queries, chains=chains),
        grid=(batch, n_heads, n_lat),
        in_specs=in_specs,
        out_specs=out_specs,
        out_shape=out_shape,
        compiler_params=_params("parallel", "parallel", "arbitrary"),
        name="diff_attention",
    )(*args)
    return (out[0], out[1]) if with_ctx_queries else (out[0], None)


def _conv_kernel(prev_ref, cur_ref, next_ref, dw_ref, dwb_ref, g_ref, b_ref, o_ref,
                 xs_ref, sh_ref, y_ref, *, rows_tile, width, seq, ctx_len, t_lat):
    i = pl.program_id(0)
    r0 = i * rows_tile
    is_lat = r0 < t_lat
    per_seq = jnp.where(is_lat, seq // rows_tile, ctx_len // rows_tile)
    pos = jnp.where(is_lat, i % (seq // rows_tile), (i - t_lat // rows_tile) % (ctx_len // rows_tile))
    keep_prev = (pos > 0).astype(F32)
    keep_next = (pos < per_seq - 1).astype(F32)
    h = CONV_HALO
    xs_ref[0:h, :] = jnp.where(keep_prev > 0, prev_ref[...], 0.0)
    xs_ref[h:h + rows_tile, :] = cur_ref[...]
    xs_ref[h + rows_tile:2 * h + rows_tile, :] = jnp.where(keep_next > 0, next_ref[...], 0.0)

    base = h - width // 2
    n_ch = cur_ref.shape[1]
    sh_rows = sh_ref.shape[1]
    for s in range(1, SUBLANES):
        sh_ref[s - 1] = xs_ref[s:s + sh_rows, :]

    def lane_chunk(c, carry):
        cols = pl.ds(pl.multiple_of(c * LANES, LANES), LANES)
        taps = [jnp.broadcast_to(dw_ref[t:t + 1, cols], (SUBLANES, LANES)) for t in range(width)]
        bias = jnp.broadcast_to(dwb_ref[:, cols], (SUBLANES, LANES))
        for row in range(0, rows_tile, SUBLANES):
            acc = bias
            for t in range(width):
                s, r0 = (base + t) % SUBLANES, row + (base + t) // SUBLANES * SUBLANES
                src = xs_ref[r0:r0 + SUBLANES, cols] if s == 0 else sh_ref[s - 1, r0:r0 + SUBLANES, cols]
                acc = acc + src * taps[t]
            y_ref[row:row + SUBLANES, cols] = acc
        return carry

    lax.fori_loop(0, n_ch // LANES, lane_chunk, 0)

    y = _layer_norm_rows(y_ref[...]) * g_ref[...] + b_ref[...]
    o_ref[...] = (y * _sigmoid(y)).astype(o_ref.dtype)


def _conv_call(u, dw, dw_b, ln_g, ln_b, rows, seq, ctx_len, t_lat, tile):
    width, n_ch = dw.shape
    assert width // 2 < CONV_HALO and tile % CONV_HALO == 0
    hb = tile // CONV_HALO
    n_halo_blocks = rows // CONV_HALO
    dw_pad = jnp.zeros((2 * CONV_HALO, n_ch), F32).at[:width].set(dw)
    vec = pl.BlockSpec((1, n_ch), lambda i: (0, 0))
    sh_rows = tile + (CONV_HALO - width // 2 + width - 1) // SUBLANES * SUBLANES
    assert SUBLANES - 1 + sh_rows <= tile + 2 * CONV_HALO
    return pl.pallas_call(
        functools.partial(_conv_kernel, rows_tile=tile, width=width, seq=seq, ctx_len=ctx_len,
                          t_lat=t_lat),
        grid=(rows // tile,),
        in_specs=[pl.BlockSpec((CONV_HALO, n_ch), lambda i: (jnp.maximum(i * hb - 1, 0), 0)),
                  pl.BlockSpec((tile, n_ch), lambda i: (i, 0)),
                  pl.BlockSpec((CONV_HALO, n_ch),
                               lambda i: (jnp.minimum((i + 1) * hb, n_halo_blocks - 1), 0)),
                  pl.BlockSpec((2 * CONV_HALO, n_ch), lambda i: (0, 0)), vec, vec, vec],
        out_specs=pl.BlockSpec((tile, n_ch), lambda i: (i, 0)),
        out_shape=jax.ShapeDtypeStruct((rows, n_ch), BF16),
        scratch_shapes=[pltpu.VMEM((tile + 2 * CONV_HALO, n_ch), F32),
                        pltpu.VMEM((SUBLANES - 1, sh_rows, n_ch), F32),
                        pltpu.VMEM((tile, n_ch), F32)],
        compiler_params=_params("parallel"),
        name="conformer_conv",
    )(u, u, u, dw_pad, dw_b.reshape(1, n_ch), ln_g.reshape(1, n_ch), ln_b.reshape(1, n_ch))


def kernel(x, c, ctx, c_ctx, w_mod, b_mod, ln_g, ln_b, ffn1_w_gate, ffn1_w_up, ffn1_w_down,
           w_in, b_glu, conv_dw, conv_dw_b, conv_ln_g, conv_ln_b, lambda_q1, lambda_k1,
           lambda_q2, lambda_k2, attn_subln_g, w_out, ffn2_w_gate, ffn2_w_up, ffn2_w_down):
    batch, seq, d = x.shape
    ctx_len = ctx.shape[1]
    depth = w_mod.shape[0]
    d_ff = ffn1_w_gate.shape[2]
    conv_ch = conv_dw.shape[2]
    d_head = lambda_q1.shape[1]
    dv = attn_subln_g.shape[1]
    attn_w = w_out.shape[1] - conv_ch
    n_heads = attn_w // dv
    t_lat, t_ctx = batch * seq, batch * ctx_len
    t_all = t_lat + t_ctx
    alpha = (2 * depth) ** 0.25
    assert batch + 1 <= MOD_ROWS and dv == 2 * d_head and t_lat % ctx_len == 0

    unit = math.gcd(seq, t_ctx)
    tm = _pick(unit, 1024)
    tm_ln = _pick(unit, 256)
    tq = _pick(seq, 1024)
    conv_tile = _pick(math.gcd(seq, ctx_len), 256)

    c_rows = jnp.zeros((MOD_ROWS, d), F32).at[:batch].set(c).at[batch].set(c_ctx)
    mods = _mod_call(c_rows, w_mod, b_mod).reshape(depth, MOD_ROWS * N_MOD, 1, d)

    xr, xm = _entry_call(x.reshape(t_lat, d), ctx.reshape(t_ctx, d), mods[0], seq, batch, tm_ln)

    tab = _rope_tables(seq, tm, d_head)

    for i in range(depth):
        last = i == depth - 1
        lam_init = 0.8 - 0.6 * math.exp(-0.3 * i)
        mod = mods[i]
        norm = functools.partial(_postnorm_call, seq=seq, batch=batch, alpha=alpha, tm=tm_ln)

        hid = _dual_call(xm, ffn1_w_gate, 0, ffn1_w_up, 0, i, d_ff, None, "swiglu", BF16, t_all,
                         tm, _pick(d_ff, 512))
        y = _matmul_call([hid], ffn1_w_down, i, d, BF16, t_all, _pick(tm, 512), _pick(d, 512))
        xr, xm = norm(xr, y, mod, 2, 0.5, ln_g[i, 0], ln_b[i, 0], mod, (3, 4), t_all)

        rows = t_lat if last else t_all
        u = _dual_call(xm, w_in, 0, w_in, conv_ch, i, conv_ch, b_glu[i].reshape(1, -1), "glu",
                       F32, rows, tm, _pick(conv_ch, 512))
        head_proj = functools.partial(_head_proj_call, xm, w_in, i, n_out=attn_w, seq=seq,
                                      t_lat=t_lat, d_head=d_head, tm=tm, tn=_pick(attn_w, 1024))
        qt = head_proj(col_off=2 * conv_ch, tab=tab, scale=d_head ** -0.5 * LOG2_E,
                       transpose=True, rows=rows)
        kk = head_proj(col_off=2 * conv_ch + attn_w, tab=tab, scale=1.0, transpose=False,
                       rows=t_all)
        vt = head_proj(col_off=2 * conv_ch + 2 * attn_w, tab=None, scale=1.0, transpose=True,
                       rows=t_all)
        lam_vecs = jnp.zeros((4, LANES), F32).at[:, :d_head].set(
            jnp.stack([lambda_q1[i], lambda_k1[i], lambda_q2[i], lambda_k2[i]]).astype(F32))
        attn, attn_ctx = _attn_call(qt, kk, vt, lam_vecs, attn_subln_g[i], lam_init, batch, seq,
                                    ctx_len, n_heads, d_head, not last, tq)
        if not last:
            attn = jnp.concatenate([attn, attn_ctx], axis=0)
        conv = _conv_call(u, conv_dw[i], conv_dw_b[i], conv_ln_g[i], conv_ln_b[i], rows, seq,
                          ctx_len, t_lat, conv_tile)
        y = _matmul_call([conv, attn], w_out, i, d, BF16, rows, tm, _pick(d, 1024))
        xr, xm = norm(xr, y, mod, 5, 1.0, ln_g[i, 1], ln_b[i, 1], mod, (6, 7), rows)

        hid = _dual_call(xm, ffn2_w_gate, 0, ffn2_w_up, 0, i, d_ff, None, "swiglu", BF16, rows,
                         tm, _pick(d_ff, 512))
        y = _matmul_call([hid], ffn2_w_down, i, d, BF16, rows, _pick(tm, 512), _pick(d, 512))
        nxt = None if last else mods[i + 1]
        xr, xm = norm(xr, y, mod, 8, 0.5, ln_g[i, 2], ln_b[i, 2], nxt, (0, 1), rows)

    return xr.reshape(batch, seq, d)
```

```python
import functools
import math
from typing import NamedTuple

import jax
import jax.numpy as jnp
import numpy as np
from jax import lax
from jax.experimental import pallas as pl
from jax.experimental.pallas import tpu as pltpu

GRID_W = 64
ROPE_BASE = 10000.0
LN_EPS = 1e-5
N_MOD = 9
LANES = 128
SUBLANES = 8
BF16_ROWS = 16
MOD_ROWS = 8
ATTN_KEY_CHUNK = 1024
ATTN_CHAINS = 8
LOG2_E = 1.4426950408889634
CONV_HALO = 16
VMEM_LIMIT_BYTES = 56 * 1024 * 1024

BF16 = jnp.bfloat16
F32 = jnp.float32


def _pick(n, pref):
    t = min(n, pref)
    while n % t:
        t //= 2
    return t


class _Tiles(NamedTuple):
    rows: int
    norm_rows: int
    conv_rows: int
    queries: int
    ffn_cols: int
    glu_cols: int
    down_cols: int
    out_cols: int
    head_cols: int


def _tile_plan(seq, ctx_len, t_ctx, d, d_ff, conv_ch, attn_w):
    unit = math.gcd(seq, t_ctx)
    return _Tiles(rows=_pick(unit, 1024), norm_rows=_pick(unit, 256),
                  conv_rows=_pick(math.gcd(seq, ctx_len), 256), queries=_pick(seq, 1024),
                  ffn_cols=_pick(d_ff, 512), glu_cols=_pick(conv_ch, 512),
                  down_cols=_pick(d, 512), out_cols=_pick(d, 1024),
                  head_cols=_pick(attn_w, 1024))


def _params(*sem):
    return pltpu.CompilerParams(dimension_semantics=sem, vmem_limit_bytes=VMEM_LIMIT_BYTES)


def _sigmoid(v):
    return 1.0 / (1.0 + jnp.exp(-v))


def _group_of(row0, seq, batch):
    return jnp.minimum(row0 // seq, batch)


def _mod_spec(d, tm, seq, batch, slot, order):
    def idx(*g):
        i = g[order]
        return (_group_of(i * tm, seq, batch) * N_MOD + slot, 0, 0)
    return pl.BlockSpec((1, 1, d), idx)


def _mod_kernel(c_ref, w_ref, b_ref, o_ref):
    c = c_ref[...]
    s = (c * _sigmoid(c)).astype(BF16)
    w = w_ref[0].astype(BF16)
    o_ref[0] = jnp.dot(s, w, preferred_element_type=F32) + b_ref[0]


def _mod_call(c_rows, w_mod, b_mod):
    depth, d, n = w_mod.shape
    tn = _pick(n, 512)
    return pl.pallas_call(
        _mod_kernel,
        grid=(depth, n // tn),
        in_specs=[pl.BlockSpec((MOD_ROWS, d), lambda l, j: (0, 0)),
                  pl.BlockSpec((1, d, tn), lambda l, j: (l, 0, j)),
                  pl.BlockSpec((1, 1, tn), lambda l, j: (l, 0, j))],
        out_specs=pl.BlockSpec((1, MOD_ROWS, tn), lambda l, j: (l, 0, j)),
        out_shape=jax.ShapeDtypeStruct((depth, MOD_ROWS, n), F32),
        compiler_params=_params("parallel", "parallel"),
        name="adaln_mod",
    )(c_rows, w_mod, b_mod.reshape(depth, 1, n))


def _layer_norm_rows(z):
    mu = jnp.mean(z, axis=-1, keepdims=True)
    zc = z - mu
    var = jnp.mean(zc * zc, axis=-1, keepdims=True)
    return zc * lax.rsqrt(var + LN_EPS)


def _entry_kernel(x_ref, c_ref, sh_ref, sc_ref, xn_ref, xm_ref, *, n_lat_tiles):
    def emit(src_ref):
        y = _layer_norm_rows(src_ref[...])
        xn_ref[...] = y
        xm_ref[...] = (y * (1.0 + sc_ref[0]) + sh_ref[0]).astype(BF16)

    i = pl.program_id(0)
    pl.when(i < n_lat_tiles)(lambda: emit(x_ref))
    pl.when(i >= n_lat_tiles)(lambda: emit(c_ref))


def _entry_call(x2, ctx2, mod, seq, batch, tm):
    (t_lat, d), t_ctx = x2.shape, ctx2.shape[0]
    n_lat = t_lat // tm
    t = t_lat + t_ctx
    row = pl.BlockSpec((tm, d), lambda i: (i, 0))
    return pl.pallas_call(
        functools.partial(_entry_kernel, n_lat_tiles=n_lat),
        grid=(t // tm,),
        in_specs=[pl.BlockSpec((tm, d), lambda i: (jnp.minimum(i, n_lat - 1), 0)),
                  pl.BlockSpec((tm, d), lambda i: (jnp.maximum(i - n_lat, 0), 0)),
                  _mod_spec(d, tm, seq, batch, 0, 0), _mod_spec(d, tm, seq, batch, 1, 0)],
        out_specs=[row, row],
        out_shape=[jax.ShapeDtypeStruct((t, d), F32), jax.ShapeDtypeStruct((t, d), BF16)],
        compiler_params=_params("parallel"),
        name="entry_norm",
    )(x2, ctx2, mod, mod)


def _postnorm_kernel(*refs, alpha, coef, with_next):
    if with_next:
        x_ref, y_ref, gate_ref, g_ref, b_ref, sh_ref, sc_ref, xn_ref, xm_ref = refs
    else:
        x_ref, y_ref, gate_ref, g_ref, b_ref, xn_ref = refs
    z = alpha * x_ref[...] + (coef * gate_ref[0]) * y_ref[...]
    y = _layer_norm_rows(z) * g_ref[...] + b_ref[...]
    xn_ref[...] = y
    if with_next:
        xm_ref[...] = (y * (1.0 + sc_ref[0]) + sh_ref[0]).astype(BF16)


def _postnorm_call(x, y, mod_gate, gate_slot, coef, g, b, mod_next, next_slots,
                   rows, seq, batch, alpha, tm):
    d = x.shape[1]
    row = pl.BlockSpec((tm, d), lambda i: (i, 0))
    vec = pl.BlockSpec((1, d), lambda i: (0, 0))
    with_next = mod_next is not None
    in_specs = [row, row, _mod_spec(d, tm, seq, batch, gate_slot, 0), vec, vec]
    args = [x, y, mod_gate, g.reshape(1, d), b.reshape(1, d)]
    out_specs = [row]
    out_shape = [jax.ShapeDtypeStruct((rows, d), F32)]
    if with_next:
        in_specs += [_mod_spec(d, tm, seq, batch, next_slots[0], 0),
                     _mod_spec(d, tm, seq, batch, next_slots[1], 0)]
        args += [mod_next, mod_next]
        out_specs.append(row)
        out_shape.append(jax.ShapeDtypeStruct((rows, d), BF16))
    out = pl.pallas_call(
        functools.partial(_postnorm_kernel, alpha=alpha, coef=coef, with_next=with_next),
        grid=(rows // tm,),
        in_specs=in_specs,
        out_specs=out_specs,
        out_shape=out_shape,
        compiler_params=_params("parallel"),
        name="post_norm",
    )(*args)
    return (out[0], out[1]) if with_next else (out[0], None)


def _stream_chunks(k, n_row_tiles):
    n = 1
    while n * 2 <= n_row_tiles and k % (n * 2 * BF16_ROWS) == 0:
        n *= 2
    return n


def _streamed_weight_spec(k, tn, n_chunks, n_sweeps, layer, row_off, col_off):
    kb = k // n_chunks
    assert row_off % kb == 0 and col_off % tn == 0

    def idx(s, i):
        return (layer, row_off // kb + jnp.minimum(i, n_chunks - 1),
                col_off // tn + jnp.minimum(s, n_sweeps - 1))
    return pl.BlockSpec((None, kb, tn), idx)


def _stage_weights(w_refs, wb_refs, n_chunks, n_sweeps):
    s, i = pl.program_id(0), pl.program_id(1)

    @pl.when(jnp.logical_and(i < n_chunks, s < n_sweeps))
    def _():
        for w_ref, wb_ref in zip(w_refs, wb_refs):
            kb = w_ref.shape[0]
            wb_ref[s % 2, pl.ds(pl.multiple_of(i * kb, kb), kb), :] = w_ref[...].astype(BF16)


def _row_tile(s, i):
    return jnp.where(s == 0, 0, i)


def _col_block(s):
    return jnp.maximum(s - 1, 0)


def _dual_kernel(*refs, act, with_bias, n_chunks, n_sweeps):
    if with_bias:
        a_ref, w1_ref, w2_ref, b1_ref, b2_ref, o_ref, w1b_ref, w2b_ref = refs
    else:
        a_ref, w1_ref, w2_ref, o_ref, w1b_ref, w2b_ref = refs
    _stage_weights([w1_ref, w2_ref], [w1b_ref, w2b_ref], n_chunks, n_sweeps)
    sweep = pl.program_id(0)
    slot = (sweep + 1) % 2

    @pl.when(sweep > 0)
    def _():
        a = a_ref[...]
        p1 = jnp.dot(a, w1b_ref[slot], preferred_element_type=F32)
        p2 = jnp.dot(a, w2b_ref[slot], preferred_element_type=F32)
        if with_bias:
            p1 = p1 + b1_ref[...]
            p2 = p2 + b2_ref[...]
        if act == "swiglu":
            r = (p1 * _sigmoid(p1)) * p2
        else:
            r = p1 * _sigmoid(p2)
        o_ref[...] = r.astype(o_ref.dtype)


def _dual_call(a, w1, off1, w2, off2, layer, n_out, bias, act, out_dtype, rows, tm, tn):
    k = a.shape[1]
    with_bias = bias is not None
    n_sweeps, n_tiles = n_out // tn, rows // tm
    n_chunks = _stream_chunks(k, n_tiles)
    in_specs = [pl.BlockSpec((tm, k), lambda s, i: (_row_tile(s, i), 0)),
                _streamed_weight_spec(k, tn, n_chunks, n_sweeps, layer, 0, off1),
                _streamed_weight_spec(k, tn, n_chunks, n_sweeps, layer, 0, off2)]
    args = [a, w1, w2]
    if with_bias:
        in_specs += [pl.BlockSpec((1, tn), lambda s, i: (0, _col_block(s) + off1 // tn)),
                     pl.BlockSpec((1, tn), lambda s, i: (0, _col_block(s) + off2 // tn))]
        args += [bias, bias]
    return pl.pallas_call(
        functools.partial(_dual_kernel, act=act, with_bias=with_bias, n_chunks=n_chunks,
                          n_sweeps=n_sweeps),
        grid=(n_sweeps + 1, n_tiles),
        in_specs=in_specs,
        out_specs=pl.BlockSpec((tm, tn), lambda s, i: (_row_tile(s, i), _col_block(s))),
        out_shape=jax.ShapeDtypeStruct((rows, n_out), out_dtype),
        scratch_shapes=[pltpu.VMEM((2, k, tn), BF16), pltpu.VMEM((2, k, tn), BF16)],
        compiler_params=_params("arbitrary", "arbitrary"),
        name="gated_proj_" + act,
    )(*args)


def _matmul_kernel(*refs, n_pairs, n_chunks, n_sweeps):
    a_refs, w_refs = refs[:n_pairs], refs[n_pairs:2 * n_pairs]
    o_ref, wb_refs = refs[2 * n_pairs], refs[2 * n_pairs + 1:]
    _stage_weights(w_refs, wb_refs, n_chunks, n_sweeps)
    sweep = pl.program_id(0)
    slot = (sweep + 1) % 2

    @pl.when(sweep > 0)
    def _():
        acc = jnp.dot(a_refs[0][...], wb_refs[0][slot], preferred_element_type=F32)
        for a_ref, wb_ref in zip(a_refs[1:], wb_refs[1:]):
            acc = acc + jnp.dot(a_ref[...], wb_ref[slot], preferred_element_type=F32)
        o_ref[...] = acc.astype(o_ref.dtype)


def _matmul_call(a_list, w, layer, n_out, out_dtype, rows, tm, tn):
    n_pairs = len(a_list)
    n_sweeps, n_tiles = n_out // tn, rows // tm
    n_chunks = min(_stream_chunks(a.shape[1], n_tiles) for a in a_list)
    in_specs, w_specs, scratch, row_off = [], [], [], 0
    for a in a_list:
        kp = a.shape[1]
        in_specs.append(pl.BlockSpec((tm, kp), lambda s, i: (_row_tile(s, i), 0)))
        w_specs.append(_streamed_weight_spec(kp, tn, n_chunks, n_sweeps, layer, row_off, 0))
        scratch.append(pltpu.VMEM((2, kp, tn), BF16))
        row_off += kp
    return pl.pallas_call(
        functools.partial(_matmul_kernel, n_pairs=n_pairs, n_chunks=n_chunks, n_sweeps=n_sweeps),
        grid=(n_sweeps + 1, n_tiles),
        in_specs=in_specs + w_specs,
        out_specs=pl.BlockSpec((tm, tn), lambda s, i: (_row_tile(s, i), _col_block(s))),
        out_shape=jax.ShapeDtypeStruct((rows, n_out), out_dtype),
        scratch_shapes=scratch,
        compiler_params=_params("arbitrary", "arbitrary"),
        name="proj",
    )(*a_list, *([w] * n_pairs))


def _rope_tables(seq, tm, d_head):
    axis_dim = d_head // 2
    half = axis_dim // 2
    inv = (1.0 / (ROPE_BASE ** (np.arange(half, dtype=np.float32) / half))).astype(np.float32)
    pos = np.arange(seq)
    r = (pos // GRID_W).astype(np.float32)
    col = (pos % GRID_W).astype(np.float32)
    ang_r = r[:, None] * inv[None, :]
    ang_c = col[:, None] * inv[None, :]
    ang = np.concatenate([ang_r, ang_r, ang_c, ang_c], axis=-1)
    ang = np.tile(ang, (1, LANES // d_head))
    first_half = (np.arange(LANES) % axis_dim) < half
    cos, sin = np.cos(ang), np.sin(ang)
    sin_up = np.where(first_half[None, :], -sin, 0.0)
    sin_dn = np.where(first_half[None, :], 0.0, sin)
    ident = np.stack([np.ones((tm, LANES)), np.zeros((tm, LANES)), np.zeros((tm, LANES))])
    tab = np.concatenate([np.stack([cos, sin_up, sin_dn]), ident], axis=1)
    return jnp.asarray(tab.astype(np.float32))


def _head_proj_kernel(*refs, rope, half, scale, transpose, n_chunks, n_sweeps):
    if rope:
        a_ref, w_ref, tab_ref, o_ref, wb_ref = refs
    else:
        a_ref, w_ref, o_ref, wb_ref = refs
    _stage_weights([w_ref], [wb_ref], n_chunks, n_sweeps)
    sweep = pl.program_id(0)
    slot = (sweep + 1) % 2

    @pl.when(sweep > 0)
    def _():
        acc = jnp.dot(a_ref[...], wb_ref[slot], preferred_element_type=F32)
        for c in range(acc.shape[1] // LANES):
            cols = slice(c * LANES, (c + 1) * LANES)
            r = acc[:, cols]
            if rope:
                up = pltpu.roll(r, LANES - half, 1)
                dn = pltpu.roll(r, half, 1)
                r = r * tab_ref[0] + up * tab_ref[1] + dn * tab_ref[2]
            if scale != 1.0:
                r = r * scale
            if transpose:
                o_ref[cols, :] = r.T.astype(o_ref.dtype)
            else:
                o_ref[:, cols] = r.astype(o_ref.dtype)


def _head_proj_call(a, w, layer, col_off, n_out, tab, seq, t_lat, d_head, scale, transpose, rows,
                    tm, tn):
    k = a.shape[1]
    rope = tab is not None
    n_sweeps, n_tiles = n_out // tn, rows // tm
    n_chunks = _stream_chunks(k, n_tiles)

    def tab_idx(s, i):
        r0 = _row_tile(s, i) * tm
        return (0, jnp.where(r0 < t_lat, (r0 % seq) // tm, seq // tm), 0)

    in_specs = [pl.BlockSpec((tm, k), lambda s, i: (_row_tile(s, i), 0)),
                _streamed_weight_spec(k, tn, n_chunks, n_sweeps, layer, 0, col_off)]
    args = [a, w]
    if rope:
        in_specs.append(pl.BlockSpec((3, tm, LANES), tab_idx))
        args.append(tab)
    if transpose:
        out_spec = pl.BlockSpec((tn, tm), lambda s, i: (_col_block(s), _row_tile(s, i)))
        out_shape = jax.ShapeDtypeStruct((n_out, rows), BF16)
    else:
        out_spec = pl.BlockSpec((tm, tn), lambda s, i: (_row_tile(s, i), _col_block(s)))
        out_shape = jax.ShapeDtypeStruct((rows, n_out), BF16)
    return pl.pallas_call(
        functools.partial(_head_proj_kernel, rope=rope, half=d_head // 4, scale=scale,
                          transpose=transpose, n_chunks=n_chunks, n_sweeps=n_sweeps),
        grid=(n_sweeps + 1, n_tiles),
        in_specs=in_specs,
        out_specs=out_spec,
        out_shape=out_shape,
        scratch_shapes=[pltpu.VMEM((2, k, tn), BF16)],
        compiler_params=_params("arbitrary", "arbitrary"),
        name="head_proj",
    )(*args)


NEG_BIG = -1e30


def _diff_attend(qts, segments, lam, gain, d_head, key_chunk):
    chunks = [(k_ref, vt_ref, slice(c0, min(c0 + key_chunk, k_ref.shape[0])))
              for k_ref, vt_ref in segments for c0 in range(0, k_ref.shape[0], key_chunk)]

    def scores(qmt, chunk):
        return jnp.dot(chunk[0][chunk[2], :], qmt, preferred_element_type=F32)

    state = []
    for qt in qts:
        n = qt.shape[1]
        row = lax.broadcasted_iota(jnp.int32, qt.shape, 0)
        zero = jnp.zeros_like(qt)
        qmt = jnp.concatenate([jnp.where(row < d_head, qt, zero),
                               jnp.where(row >= d_head, qt, zero)], axis=1)
        state.append(dict(qmt=qmt, m=jnp.full((1, 2 * n), NEG_BIG, F32),
                          l=jnp.zeros((1, 2 * n), F32),
                          acc=jnp.zeros((qt.shape[0], 2 * n), F32),
                          s_next=scores(qmt, chunks[0])))
    for i, (k_ref, vt_ref, keys) in enumerate(chunks):
        for st in state:
            s = st["s_next"]
            st["s_next"] = scores(st["qmt"], chunks[i + 1]) if i + 1 < len(chunks) else None
            m_new = jnp.maximum(st["m"], jnp.max(s, axis=0, keepdims=True))
            alpha = jnp.exp2(st["m"] - m_new)
            p = jnp.exp2(s - m_new)
            st["l"] = st["l"] * alpha + jnp.sum(p, axis=0, keepdims=True)
            st["acc"] = st["acc"] * alpha + jnp.dot(vt_ref[:, keys], p.astype(BF16),
                                                    preferred_element_type=F32)
            st["m"] = m_new
    outs = []
    for st in state:
        n = st["qmt"].shape[1] // 2
        o = st["acc"] * (1.0 / st["l"])
        d = (o[:, :n] - lam * o[:, n:]).T
        outs.append(d * lax.rsqrt(jnp.mean(d * d, axis=-1, keepdims=True) + LN_EPS) * gain)
    return outs


def _attn_kernel(*refs, lam_init, d_head, key_chunk, with_ctx_queries, chains):
    if with_ctx_queries:
        lam_ref, g_ref, qt_ref, kc_ref, vtc_ref, kl_ref, vtl_ref, qtc_ref, o_ref, oc_ref = refs
    else:
        lam_ref, g_ref, qt_ref, kc_ref, vtc_ref, kl_ref, vtl_ref, o_ref = refs
    lv = lam_ref[...]
    lam = (jnp.exp(jnp.sum(lv[0:1] * lv[1:2], axis=-1, keepdims=True))
           - jnp.exp(jnp.sum(lv[2:3] * lv[3:4], axis=-1, keepdims=True)) + lam_init)
    gain = g_ref[...] * (1.0 - lam_init)

    n = qt_ref.shape[1] // chains
    groups = [slice(g * n, (g + 1) * n) for g in range(chains)]
    outs = _diff_attend([qt_ref[:, g] for g in groups], [(kc_ref, vtc_ref), (kl_ref, vtl_ref)],
                        lam, gain, d_head, key_chunk)
    for g, o in zip(groups, outs):
        o_ref[g, :] = o.astype(o_ref.dtype)

    if with_ctx_queries:
        @pl.when(pl.program_id(2) == 0)
        def _():
            oc, = _diff_attend([qtc_ref[...]], [(kc_ref, vtc_ref)], lam, gain, d_head, key_chunk)
            oc_ref[...] = oc.astype(oc_ref.dtype)


def _attn_call(qt, kk, vt, lam_vecs, subln_g, lam_init, batch, seq, ctx_len, n_heads, d_head,
               with_ctx_queries, tq, key_chunk=ATTN_KEY_CHUNK, chains=None):
    dv = 2 * d_head
    assert dv == LANES
    t_lat = batch * seq
    n_lat = seq // tq
    ctx_blk = t_lat // ctx_len
    if chains is None:
        chains = min(ATTN_CHAINS, tq // LANES)

    full = lambda shape: pl.BlockSpec(shape, lambda b, h, qi: (0, 0))
    in_specs = [full((4, LANES)), full((1, dv)),
                pl.BlockSpec((dv, tq), lambda b, h, qi: (h, b * n_lat + qi)),
                pl.BlockSpec((ctx_len, dv), lambda b, h, qi: (ctx_blk + b, h)),
                pl.BlockSpec((dv, ctx_len), lambda b, h, qi: (h, ctx_blk + b)),
                pl.BlockSpec((seq, dv), lambda b, h, qi: (b, h)),
                pl.BlockSpec((dv, seq), lambda b, h, qi: (h, b))]
    args = [lam_vecs, subln_g.reshape(1, dv), qt, kk, vt, kk, vt]
    out_specs = [pl.BlockSpec((tq, dv), lambda b, h, qi: (b * n_lat + qi, h))]
    out_shape = [jax.ShapeDtypeStruct((t_lat, n_heads * dv), BF16)]
    if with_ctx_queries:
        in_specs.append(pl.BlockSpec((dv, ctx_len), lambda b, h, qi: (h, ctx_blk + b)))
        args.append(qt)
        out_specs.append(pl.BlockSpec((ctx_len, dv), lambda b, h, qi: (b, h)))
        out_shape.append(jax.ShapeDtypeStruct((batch * ctx_len, n_heads * dv), BF16))
    out = pl.pallas_call(
        functools.partial(_attn_kernel, lam_init=lam_init, d_head=d_head, key_chunk=key_chunk,
                          with_ctx_queries=with_ctx_queries, chains=chains),
        grid=(batch, n_heads, n_lat),
        in_specs=in_specs,
        out_specs=out_specs,
        out_shape=out_shape,
        compiler_params=_params("parallel", "parallel", "arbitrary"),
        name="diff_attention",
    )(*args)
    return (out[0], out[1]) if with_ctx_queries else (out[0], None)


def _conv_kernel(prev_ref, cur_ref, next_ref, dw_ref, dwb_ref, g_ref, b_ref, o_ref,
                 xs_ref, sh_ref, y_ref, *, rows_tile, width, seq, ctx_len, t_lat):
    i = pl.program_id(0)
    r0 = i * rows_tile
    is_lat = r0 < t_lat
    per_seq = jnp.where(is_lat, seq // rows_tile, ctx_len // rows_tile)
    pos = jnp.where(is_lat, i % (seq // rows_tile), (i - t_lat // rows_tile) % (ctx_len // rows_tile))
    keep_prev = (pos > 0).astype(F32)
    keep_next = (pos < per_seq - 1).astype(F32)
    h = CONV_HALO
    xs_ref[0:h, :] = jnp.where(keep_prev > 0, prev_ref[...], 0.0)
    xs_ref[h:h + rows_tile, :] = cur_ref[...]
    xs_ref[h + rows_tile:2 * h + rows_tile, :] = jnp.where(keep_next > 0, next_ref[...], 0.0)

    base = h - width // 2
    n_ch = cur_ref.shape[1]
    sh_rows = sh_ref.shape[1]
    for s in range(1, SUBLANES):
        sh_ref[s - 1] = xs_ref[s:s + sh_rows, :]

    def lane_chunk(c, carry):
        cols = pl.ds(pl.multiple_of(c * LANES, LANES), LANES)
        taps = [jnp.broadcast_to(dw_ref[t:t + 1, cols], (SUBLANES, LANES)) for t in range(width)]
        bias = jnp.broadcast_to(dwb_ref[:, cols], (SUBLANES, LANES))
        for row in range(0, rows_tile, SUBLANES):
            acc = bias
            for t in range(width):
                s, r0 = (base + t) % SUBLANES, row + (base + t) // SUBLANES * SUBLANES
                src = xs_ref[r0:r0 + SUBLANES, cols] if s == 0 else sh_ref[s - 1, r0:r0 + SUBLANES, cols]
                acc = acc + src * taps[t]
            y_ref[row:row + SUBLANES, cols] = acc
        return carry

    lax.fori_loop(0, n_ch // LANES, lane_chunk, 0)

    y = _layer_norm_rows(y_ref[...]) * g_ref[...] + b_ref[...]
    o_ref[...] = (y * _sigmoid(y)).astype(o_ref.dtype)


def _conv_call(u, dw, dw_b, ln_g, ln_b, rows, seq, ctx_len, t_lat, tile):
    width, n_ch = dw.shape
    assert width // 2 < CONV_HALO and tile % CONV_HALO == 0
    hb = tile // CONV_HALO
    n_halo_blocks = rows // CONV_HALO
    dw_pad = jnp.zeros((2 * CONV_HALO, n_ch), F32).at[:width].set(dw)
    vec = pl.BlockSpec((1, n_ch), lambda i: (0, 0))
    sh_rows = tile + (CONV_HALO - width // 2 + width - 1) // SUBLANES * SUBLANES
    assert SUBLANES - 1 + sh_rows <= tile + 2 * CONV_HALO
    return pl.pallas_call(
        functools.partial(_conv_kernel, rows_tile=tile, width=width, seq=seq, ctx_len=ctx_len,
                          t_lat=t_lat),
        grid=(rows // tile,),
        in_specs=[pl.BlockSpec((CONV_HALO, n_ch), lambda i: (jnp.maximum(i * hb - 1, 0), 0)),
                  pl.BlockSpec((tile, n_ch), lambda i: (i, 0)),
                  pl.BlockSpec((CONV_HALO, n_ch),
                               lambda i: (jnp.minimum((i + 1) * hb, n_halo_blocks - 1), 0)),
                  pl.BlockSpec((2 * CONV_HALO, n_ch), lambda i: (0, 0)), vec, vec, vec],
        out_specs=pl.BlockSpec((tile, n_ch), lambda i: (i, 0)),
        out_shape=jax.ShapeDtypeStruct((rows, n_ch), BF16),
        scratch_shapes=[pltpu.VMEM((tile + 2 * CONV_HALO, n_ch), F32),
                        pltpu.VMEM((SUBLANES - 1, sh_rows, n_ch), F32),
                        pltpu.VMEM((tile, n_ch), F32)],
        compiler_params=_params("parallel"),
        name="conformer_conv",
    )(u, u, u, dw_pad, dw_b.reshape(1, n_ch), ln_g.reshape(1, n_ch), ln_b.reshape(1, n_ch))


def kernel(x, c, ctx, c_ctx, w_mod, b_mod, ln_g, ln_b, ffn1_w_gate, ffn1_w_up, ffn1_w_down,
           w_in, b_glu, conv_dw, conv_dw_b, conv_ln_g, conv_ln_b, lambda_q1, lambda_k1,
           lambda_q2, lambda_k2, attn_subln_g, w_out, ffn2_w_gate, ffn2_w_up, ffn2_w_down):
    batch, seq, d = x.shape
    ctx_len = ctx.shape[1]
    depth = w_mod.shape[0]
    d_ff = ffn1_w_gate.shape[2]
    conv_ch = conv_dw.shape[2]
    d_head = lambda_q1.shape[1]
    dv = attn_subln_g.shape[1]
    attn_w = w_out.shape[1] - conv_ch
    n_heads = attn_w // dv
    t_lat, t_ctx = batch * seq, batch * ctx_len
    t_all = t_lat + t_ctx
    alpha = (2 * depth) ** 0.25
    assert batch + 1 <= MOD_ROWS and dv == 2 * d_head and t_lat % ctx_len == 0

    t = _tile_plan(seq, ctx_len, t_ctx, d, d_ff, conv_ch, attn_w)

    c_rows = jnp.zeros((MOD_ROWS, d), F32).at[:batch].set(c).at[batch].set(c_ctx)
    mods = _mod_call(c_rows, w_mod, b_mod).reshape(depth, MOD_ROWS * N_MOD, 1, d)

    xr, xm = _entry_call(x.reshape(t_lat, d), ctx.reshape(t_ctx, d), mods[0], seq, batch, t.norm_rows)

    tab = _rope_tables(seq, t.rows, d_head)

    for i in range(depth):
        last = i == depth - 1
        lam_init = 0.8 - 0.6 * math.exp(-0.3 * i)
        mod = mods[i]
        norm = functools.partial(_postnorm_call, seq=seq, batch=batch, alpha=alpha, tm=t.norm_rows)

        hid = _dual_call(xm, ffn1_w_gate, 0, ffn1_w_up, 0, i, d_ff, None, "swiglu", BF16, t_all,
                         t.rows, t.ffn_cols)
        y = _matmul_call([hid], ffn1_w_down, i, d, BF16, t_all, t.rows, t.down_cols)
        xr, xm = norm(xr, y, mod, 2, 0.5, ln_g[i, 0], ln_b[i, 0], mod, (3, 4), t_all)

        rows = t_lat if last else t_all
        u = _dual_call(xm, w_in, 0, w_in, conv_ch, i, conv_ch, b_glu[i].reshape(1, -1), "glu",
                       F32, rows, t.rows, t.glu_cols)
        head_proj = functools.partial(_head_proj_call, xm, w_in, i, n_out=attn_w, seq=seq,
                                      t_lat=t_lat, d_head=d_head, tm=t.rows, tn=t.head_cols)
        qt = head_proj(col_off=2 * conv_ch, tab=tab, scale=d_head ** -0.5 * LOG2_E,
                       transpose=True, rows=rows)
        kk = head_proj(col_off=2 * conv_ch + attn_w, tab=tab, scale=1.0, transpose=False,
                       rows=t_all)
        vt = head_proj(col_off=2 * conv_ch + 2 * attn_w, tab=None, scale=1.0, transpose=True,
                       rows=t_all)
        lam_vecs = jnp.zeros((4, LANES), F32).at[:, :d_head].set(
            jnp.stack([lambda_q1[i], lambda_k1[i], lambda_q2[i], lambda_k2[i]]).astype(F32))
        attn, attn_ctx = _attn_call(qt, kk, vt, lam_vecs, attn_subln_g[i], lam_init, batch, seq,
                                    ctx_len, n_heads, d_head, not last, t.queries)
        if not last:
            attn = jnp.concatenate([attn, attn_ctx], axis=0)
        conv = _conv_call(u, conv_dw[i], conv_dw_b[i], conv_ln_g[i], conv_ln_b[i], rows, seq,
                          ctx_len, t_lat, t.conv_rows)
        y = _matmul_call([conv, attn], w_out, i, d, BF16, rows, t.rows, t.out_cols)
        xr, xm = norm(xr, y, mod, 5, 1.0, ln_g[i, 1], ln_b[i, 1], mod, (6, 7), rows)

        hid = _dual_call(xm, ffn2_w_gate, 0, ffn2_w_up, 0, i, d_ff, None, "swiglu", BF16, rows,
                         t.rows, t.ffn_cols)
        y = _matmul_call([hid], ffn2_w_down, i, d, BF16, rows, t.rows, t.down_cols)
        nxt = None if last else mods[i + 1]
        xr, xm = norm(xr, y, mod, 8, 0.5, ln_g[i, 2], ln_b[i, 2], nxt, (0, 1), rows)

    return xr.reshape(batch, seq, d)
```

```python
import functools
import math
from typing import NamedTuple

import jax
import jax.numpy as jnp
import numpy as np
from jax import lax
from jax.experimental import pallas as pl
from jax.experimental.pallas import tpu as pltpu

GRID_W = 64
ROPE_BASE = 10000.0
LN_EPS = 1e-5
N_MOD = 9
LANES = 128
SUBLANES = 8
BF16_ROWS = 16
MOD_ROWS = 8
ATTN_KEY_CHUNK = 1024
ATTN_CHAINS = 8
LOG2_E = 1.4426950408889634
CONV_HALO = 16
VMEM_LIMIT_BYTES = 56 * 1024 * 1024

BF16 = jnp.bfloat16
F32 = jnp.float32


def _pick(n, pref):
    t = min(n, pref)
    while n % t:
        t //= 2
    return t


class _Tiles(NamedTuple):
    rows: int
    norm_rows: int
    conv_rows: int
    queries: int
    ffn_cols: int
    glu_cols: int
    down_cols: int
    out_cols: int
    head_cols: int


def _tile_plan(seq, ctx_len, t_ctx, d, d_ff, conv_ch, attn_w):
    unit = math.gcd(seq, t_ctx)
    return _Tiles(rows=_pick(unit, 1024), norm_rows=_pick(unit, 256),
                  conv_rows=_pick(math.gcd(seq, ctx_len), 256), queries=_pick(seq, 1024),
                  ffn_cols=_pick(d_ff, 512), glu_cols=_pick(conv_ch, 512),
                  down_cols=_pick(d, 512), out_cols=_pick(d, 1024),
                  head_cols=_pick(attn_w, 1024))


def _params(*sem):
    return pltpu.CompilerParams(dimension_semantics=sem, vmem_limit_bytes=VMEM_LIMIT_BYTES)


def _sigmoid(v):
    return 1.0 / (1.0 + jnp.exp(-v))


def _group_of(row0, seq, batch):
    return jnp.minimum(row0 // seq, batch)


def _mod_spec(d, tm, seq, batch, slot, order):
    def idx(*g):
        i = g[order]
        return (_group_of(i * tm, seq, batch) * N_MOD + slot, 0, 0)
    return pl.BlockSpec((1, 1, d), idx)


def _mod_kernel(c_ref, w_ref, b_ref, o_ref):
    c = c_ref[...]
    s = (c * _sigmoid(c)).astype(BF16)
    w = w_ref[0].astype(BF16)
    o_ref[0] = jnp.dot(s, w, preferred_element_type=F32) + b_ref[0]


def _mod_call(c_rows, w_mod, b_mod):
    depth, d, n = w_mod.shape
    tn = _pick(n, 512)
    return pl.pallas_call(
        _mod_kernel,
        grid=(depth, n // tn),
        in_specs=[pl.BlockSpec((MOD_ROWS, d), lambda l, j: (0, 0)),
                  pl.BlockSpec((1, d, tn), lambda l, j: (l, 0, j)),
                  pl.BlockSpec((1, 1, tn), lambda l, j: (l, 0, j))],
        out_specs=pl.BlockSpec((1, MOD_ROWS, tn), lambda l, j: (l, 0, j)),
        out_shape=jax.ShapeDtypeStruct((depth, MOD_ROWS, n), F32),
        compiler_params=_params("parallel", "parallel"),
        name="adaln_mod",
    )(c_rows, w_mod, b_mod.reshape(depth, 1, n))


def _layer_norm_rows(z):
    mu = jnp.mean(z, axis=-1, keepdims=True)
    zc = z - mu
    var = jnp.mean(zc * zc, axis=-1, keepdims=True)
    return zc * lax.rsqrt(var + LN_EPS)


def _entry_kernel(x_ref, c_ref, sh_ref, sc_ref, xn_ref, xm_ref, *, n_lat_tiles):
    def emit(src_ref):
        y = _layer_norm_rows(src_ref[...])
        xn_ref[...] = y
        xm_ref[...] = (y * (1.0 + sc_ref[0]) + sh_ref[0]).astype(BF16)

    i = pl.program_id(0)
    pl.when(i < n_lat_tiles)(lambda: emit(x_ref))
    pl.when(i >= n_lat_tiles)(lambda: emit(c_ref))


def _entry_call(x2, ctx2, mod, seq, batch, tm):
    (t_lat, d), t_ctx = x2.shape, ctx2.shape[0]
    n_lat = t_lat // tm
    t = t_lat + t_ctx
    row = pl.BlockSpec((tm, d), lambda i: (i, 0))
    return pl.pallas_call(
        functools.partial(_entry_kernel, n_lat_tiles=n_lat),
        grid=(t // tm,),
        in_specs=[pl.BlockSpec((tm, d), lambda i: (jnp.minimum(i, n_lat - 1), 0)),
                  pl.BlockSpec((tm, d), lambda i: (jnp.maximum(i - n_lat, 0), 0)),
                  _mod_spec(d, tm, seq, batch, 0, 0), _mod_spec(d, tm, seq, batch, 1, 0)],
        out_specs=[row, row],
        out_shape=[jax.ShapeDtypeStruct((t, d), F32), jax.ShapeDtypeStruct((t, d), BF16)],
        compiler_params=_params("parallel"),
        name="entry_norm",
    )(x2, ctx2, mod, mod)


def _postnorm_kernel(*refs, alpha, coef, with_next):
    if with_next:
        x_ref, y_ref, gate_ref, g_ref, b_ref, sh_ref, sc_ref, xn_ref, xm_ref = refs
    else:
        x_ref, y_ref, gate_ref, g_ref, b_ref, xn_ref = refs
    z = alpha * x_ref[...] + (coef * gate_ref[0]) * y_ref[...]
    y = _layer_norm_rows(z) * g_ref[...] + b_ref[...]
    xn_ref[...] = y
    if with_next:
        xm_ref[...] = (y * (1.0 + sc_ref[0]) + sh_ref[0]).astype(BF16)


def _postnorm_call(x, y, mod_gate, gate_slot, coef, g, b, mod_next, next_slots,
                   rows, seq, batch, alpha, tm):
    d = x.shape[1]
    row = pl.BlockSpec((tm, d), lambda i: (i, 0))
    vec = pl.BlockSpec((1, d), lambda i: (0, 0))
    with_next = mod_next is not None
    in_specs = [row, row, _mod_spec(d, tm, seq, batch, gate_slot, 0), vec, vec]
    args = [x, y, mod_gate, g.reshape(1, d), b.reshape(1, d)]
    out_specs = [row]
    out_shape = [jax.ShapeDtypeStruct((rows, d), F32)]
    if with_next:
        in_specs += [_mod_spec(d, tm, seq, batch, next_slots[0], 0),
                     _mod_spec(d, tm, seq, batch, next_slots[1], 0)]
        args += [mod_next, mod_next]
        out_specs.append(row)
        out_shape.append(jax.ShapeDtypeStruct((rows, d), BF16))
    out = pl.pallas_call(
        functools.partial(_postnorm_kernel, alpha=alpha, coef=coef, with_next=with_next),
        grid=(rows // tm,),
        in_specs=in_specs,
        out_specs=out_specs,
        out_shape=out_shape,
        compiler_params=_params("parallel"),
        name="post_norm",
    )(*args)
    return (out[0], out[1]) if with_next else (out[0], None)


def _stream_chunks(k, n_row_tiles):
    n = 1
    while n * 2 <= n_row_tiles and k % (n * 2 * BF16_ROWS) == 0:
        n *= 2
    return n


def _streamed_weight_spec(k, tn, n_chunks, n_sweeps, layer, row_off, col_off):
    kb = k // n_chunks
    assert row_off % kb == 0 and col_off % tn == 0

    def idx(s, i):
        return (layer, row_off // kb + jnp.minimum(i, n_chunks - 1),
                col_off // tn + jnp.minimum(s, n_sweeps - 1))
    return pl.BlockSpec((None, kb, tn), idx)


def _stage_weights(w_refs, wb_refs, n_chunks, n_sweeps):
    s, i = pl.program_id(0), pl.program_id(1)

    @pl.when(jnp.logical_and(i < n_chunks, s < n_sweeps))
    def _():
        for w_ref, wb_ref in zip(w_refs, wb_refs):
            kb = w_ref.shape[0]
            wb_ref[s % 2, pl.ds(pl.multiple_of(i * kb, kb), kb), :] = w_ref[...].astype(BF16)


def _row_tile(s, i):
    return jnp.where(s == 0, 0, i)


def _col_block(s):
    return jnp.maximum(s - 1, 0)


def _dual_kernel(*refs, act, with_bias, n_chunks, n_sweeps):
    if with_bias:
        a_ref, w1_ref, w2_ref, b1_ref, b2_ref, o_ref, w1b_ref, w2b_ref = refs
    else:
        a_ref, w1_ref, w2_ref, o_ref, w1b_ref, w2b_ref = refs
    _stage_weights([w1_ref, w2_ref], [w1b_ref, w2b_ref], n_chunks, n_sweeps)
    sweep = pl.program_id(0)
    slot = (sweep + 1) % 2

    @pl.when(sweep > 0)
    def _():
        a = a_ref[...]
        p1 = jnp.dot(a, w1b_ref[slot], preferred_element_type=F32)
        p2 = jnp.dot(a, w2b_ref[slot], preferred_element_type=F32)
        if with_bias:
            p1 = p1 + b1_ref[...]
            p2 = p2 + b2_ref[...]
        if act == "swiglu":
            r = (p1 * _sigmoid(p1)) * p2
        else:
            r = p1 * _sigmoid(p2)
        o_ref[...] = r.astype(o_ref.dtype)


def _dual_call(a, w1, off1, w2, off2, layer, n_out, bias, act, out_dtype, rows, tm, tn):
    k = a.shape[1]
    with_bias = bias is not None
    n_sweeps, n_tiles = n_out // tn, rows // tm
    n_chunks = _stream_chunks(k, n_tiles)
    in_specs = [pl.BlockSpec((tm, k), lambda s, i: (_row_tile(s, i), 0)),
                _streamed_weight_spec(k, tn, n_chunks, n_sweeps, layer, 0, off1),
                _streamed_weight_spec(k, tn, n_chunks, n_sweeps, layer, 0, off2)]
    args = [a, w1, w2]
    if with_bias:
        in_specs += [pl.BlockSpec((1, tn), lambda s, i: (0, _col_block(s) + off1 // tn)),
                     pl.BlockSpec((1, tn), lambda s, i: (0, _col_block(s) + off2 // tn))]
        args += [bias, bias]
    return pl.pallas_call(
        functools.partial(_dual_kernel, act=act, with_bias=with_bias, n_chunks=n_chunks,
                          n_sweeps=n_sweeps),
        grid=(n_sweeps + 1, n_tiles),
        in_specs=in_specs,
        out_specs=pl.BlockSpec((tm, tn), lambda s, i: (_row_tile(s, i), _col_block(s))),
        out_shape=jax.ShapeDtypeStruct((rows, n_out), out_dtype),
        scratch_shapes=[pltpu.VMEM((2, k, tn), BF16), pltpu.VMEM((2, k, tn), BF16)],
        compiler_params=_params("arbitrary", "arbitrary"),
        name="gated_proj_" + act,
    )(*args)


def _matmul_kernel(*refs, n_pairs, n_chunks, n_sweeps):
    a_refs, w_refs = refs[:n_pairs], refs[n_pairs:2 * n_pairs]
    o_ref, wb_refs = refs[2 * n_pairs], refs[2 * n_pairs + 1:]
    _stage_weights(w_refs, wb_refs, n_chunks, n_sweeps)
    sweep = pl.program_id(0)
    slot = (sweep + 1) % 2

    @pl.when(sweep > 0)
    def _():
        acc = jnp.dot(a_refs[0][...], wb_refs[0][slot], preferred_element_type=F32)
        for a_ref, wb_ref in zip(a_refs[1:], wb_refs[1:]):
            acc = acc + jnp.dot(a_ref[...], wb_ref[slot], preferred_element_type=F32)
        o_ref[...] = acc.astype(o_ref.dtype)


def _matmul_call(a_list, w, layer, n_out, out_dtype, rows, tm, tn):
    n_pairs = len(a_list)
    n_sweeps, n_tiles = n_out // tn, rows // tm
    n_chunks = min(_stream_chunks(a.shape[1], n_tiles) for a in a_list)
    in_specs, w_specs, scratch, row_off = [], [], [], 0
    for a in a_list:
        kp = a.shape[1]
        in_specs.append(pl.BlockSpec((tm, kp), lambda s, i: (_row_tile(s, i), 0)))
        w_specs.append(_streamed_weight_spec(kp, tn, n_chunks, n_sweeps, layer, row_off, 0))
        scratch.append(pltpu.VMEM((2, kp, tn), BF16))
        row_off += kp
    return pl.pallas_call(
        functools.partial(_matmul_kernel, n_pairs=n_pairs, n_chunks=n_chunks, n_sweeps=n_sweeps),
        grid=(n_sweeps + 1, n_tiles),
        in_specs=in_specs + w_specs,
        out_specs=pl.BlockSpec((tm, tn), lambda s, i: (_row_tile(s, i), _col_block(s))),
        out_shape=jax.ShapeDtypeStruct((rows, n_out), out_dtype),
        scratch_shapes=scratch,
        compiler_params=_params("arbitrary", "arbitrary"),
        name="proj",
    )(*a_list, *([w] * n_pairs))


def _rope_tables(seq, tm, d_head):
    axis_dim = d_head // 2
    half = axis_dim // 2
    inv = (1.0 / (ROPE_BASE ** (np.arange(half, dtype=np.float32) / half))).astype(np.float32)
    pos = np.arange(seq)
    r = (pos // GRID_W).astype(np.float32)
    col = (pos % GRID_W).astype(np.float32)
    ang_r = r[:, None] * inv[None, :]
    ang_c = col[:, None] * inv[None, :]
    ang = np.concatenate([ang_r, ang_r, ang_c, ang_c], axis=-1)
    ang = np.tile(ang, (1, LANES // d_head))
    first_half = (np.arange(LANES) % axis_dim) < half
    cos, sin = np.cos(ang), np.sin(ang)
    sin_up = np.where(first_half[None, :], -sin, 0.0)
    sin_dn = np.where(first_half[None, :], 0.0, sin)
    ident = np.stack([np.ones((tm, LANES)), np.zeros((tm, LANES)), np.zeros((tm, LANES))])
    tab = np.concatenate([np.stack([cos, sin_up, sin_dn]), ident], axis=1)
    return jnp.asarray(tab.astype(np.float32))


def _head_proj_kernel(*refs, rope, half, scale, transpose, n_chunks, n_sweeps):
    if rope:
        a_ref, w_ref, tab_ref, o_ref, wb_ref = refs
    else:
        a_ref, w_ref, o_ref, wb_ref = refs
    _stage_weights([w_ref], [wb_ref], n_chunks, n_sweeps)
    sweep = pl.program_id(0)
    slot = (sweep + 1) % 2

    @pl.when(sweep > 0)
    def _():
        acc = jnp.dot(a_ref[...], wb_ref[slot], preferred_element_type=F32)
        for c in range(acc.shape[1] // LANES):
            cols = slice(c * LANES, (c + 1) * LANES)
            r = acc[:, cols]
            if rope:
                up = pltpu.roll(r, LANES - half, 1)
                dn = pltpu.roll(r, half, 1)
                r = r * tab_ref[0] + up * tab_ref[1] + dn * tab_ref[2]
            if scale != 1.0:
                r = r * scale
            if transpose:
                o_ref[cols, :] = r.T.astype(o_ref.dtype)
            else:
                o_ref[:, cols] = r.astype(o_ref.dtype)


def _head_proj_call(a, w, layer, col_off, n_out, tab, seq, t_lat, d_head, scale, transpose, rows,
                    tm, tn):
    k = a.shape[1]
    rope = tab is not None
    n_sweeps, n_tiles = n_out // tn, rows // tm
    n_chunks = _stream_chunks(k, n_tiles)

    def tab_idx(s, i):
        r0 = _row_tile(s, i) * tm
        return (0, jnp.where(r0 < t_lat, (r0 % seq) // tm, seq // tm), 0)

    in_specs = [pl.BlockSpec((tm, k), lambda s, i: (_row_tile(s, i), 0)),
                _streamed_weight_spec(k, tn, n_chunks, n_sweeps, layer, 0, col_off)]
    args = [a, w]
    if rope:
        in_specs.append(pl.BlockSpec((3, tm, LANES), tab_idx))
        args.append(tab)
    if transpose:
        out_spec = pl.BlockSpec((tn, tm), lambda s, i: (_col_block(s), _row_tile(s, i)))
        out_shape = jax.ShapeDtypeStruct((n_out, rows), BF16)
    else:
        out_spec = pl.BlockSpec((tm, tn), lambda s, i: (_row_tile(s, i), _col_block(s)))
        out_shape = jax.ShapeDtypeStruct((rows, n_out), BF16)
    return pl.pallas_call(
        functools.partial(_head_proj_kernel, rope=rope, half=d_head // 4, scale=scale,
                          transpose=transpose, n_chunks=n_chunks, n_sweeps=n_sweeps),
        grid=(n_sweeps + 1, n_tiles),
        in_specs=in_specs,
        out_specs=out_spec,
        out_shape=out_shape,
        scratch_shapes=[pltpu.VMEM((2, k, tn), BF16)],
        compiler_params=_params("arbitrary", "arbitrary"),
        name="head_proj",
    )(*args)


NEG_BIG = -1e30
BOUND_MARGIN = 1.01
BOUND_OK = 2.0 ** -80


def _max_key_sqnorm(k_refs, d_head):
    d = lax.broadcasted_iota(jnp.int32, (2 * d_head, LANES), 0)
    m = lax.broadcasted_iota(jnp.int32, (2 * d_head, LANES), 1)
    pick = jnp.where(d // d_head == m, 1.0, 0.0).astype(BF16)
    best = jnp.zeros((1, LANES), F32)
    for k_ref in k_refs:
        k = k_ref[...]
        sq = jnp.dot(k * k, pick, preferred_element_type=F32)
        best = jnp.maximum(best, jnp.max(sq, axis=0, keepdims=True))
    return best


def _diff_attend(qts, segments, lam, gain, d_head, key_chunk, key_sqnorm=None):
    bounded = key_sqnorm is not None
    chunks = [(k_ref, vt_ref, slice(c0, min(c0 + key_chunk, k_ref.shape[0])))
              for k_ref, vt_ref in segments for c0 in range(0, k_ref.shape[0], key_chunk)]

    def scores(qmt, chunk):
        return jnp.dot(chunk[0][chunk[2], :], qmt, preferred_element_type=F32)

    state = []
    for qt in qts:
        n = qt.shape[1]
        row = lax.broadcasted_iota(jnp.int32, qt.shape, 0)
        zero = jnp.zeros_like(qt)
        qmt = jnp.concatenate([jnp.where(row < d_head, qt, zero),
                               jnp.where(row >= d_head, qt, zero)], axis=1)
        if bounded:
            q2 = qmt.astype(F32)
            q2 = jnp.sum(q2 * q2, axis=0, keepdims=True)
            k2 = jnp.concatenate([jnp.broadcast_to(key_sqnorm[:, 0:1], (1, n)),
                                  jnp.broadcast_to(key_sqnorm[:, 1:2], (1, n))], axis=1)
            m = jnp.sqrt(q2 * k2) * BOUND_MARGIN
        else:
            m = jnp.full((1, 2 * n), NEG_BIG, F32)
        state.append(dict(qmt=qmt, m=m, l=jnp.zeros((1, 2 * n), F32),
                          acc=jnp.zeros((qt.shape[0], 2 * n), F32),
                          s_next=scores(qmt, chunks[0])))
    for i, (k_ref, vt_ref, keys) in enumerate(chunks):
        for st in state:
            s = st["s_next"]
            st["s_next"] = scores(st["qmt"], chunks[i + 1]) if i + 1 < len(chunks) else None
            if bounded:
                p = jnp.exp2(s - st["m"])
                st["l"] = st["l"] + jnp.sum(p, axis=0, keepdims=True)
                st["acc"] = st["acc"] + jnp.dot(vt_ref[:, keys], p.astype(BF16),
                                                preferred_element_type=F32)
            else:
                m_new = jnp.maximum(st["m"], jnp.max(s, axis=0, keepdims=True))
                alpha = jnp.exp2(st["m"] - m_new)
                p = jnp.exp2(s - m_new)
                st["l"] = st["l"] * alpha + jnp.sum(p, axis=0, keepdims=True)
                st["acc"] = st["acc"] * alpha + jnp.dot(vt_ref[:, keys], p.astype(BF16),
                                                        preferred_element_type=F32)
                st["m"] = m_new
    outs = []
    for st in state:
        n = st["qmt"].shape[1] // 2
        o = st["acc"] * (1.0 / st["l"])
        d = (o[:, :n] - lam * o[:, n:]).T
        outs.append(d * lax.rsqrt(jnp.mean(d * d, axis=-1, keepdims=True) + LN_EPS) * gain)
    return outs, functools.reduce(jnp.minimum, [jnp.min(st["l"]) for st in state])


def _attn_kernel(*refs, lam_init, d_head, key_chunk, with_ctx_queries, chains):
    if with_ctx_queries:
        (lam_ref, g_ref, qt_ref, kc_ref, vtc_ref, kl_ref, vtl_ref, qtc_ref, o_ref, oc_ref,
         ksq_ref) = refs
    else:
        lam_ref, g_ref, qt_ref, kc_ref, vtc_ref, kl_ref, vtl_ref, o_ref, ksq_ref = refs
    lv = lam_ref[...]
    lam = (jnp.exp(jnp.sum(lv[0:1] * lv[1:2], axis=-1, keepdims=True))
           - jnp.exp(jnp.sum(lv[2:3] * lv[3:4], axis=-1, keepdims=True)) + lam_init)
    gain = g_ref[...] * (1.0 - lam_init)
    first_step = pl.program_id(2) == 0

    @pl.when(first_step)
    def _():
        ksq_ref[...] = _max_key_sqnorm([kc_ref, kl_ref], d_head)

    def attend(q_groups, segments, write):
        outs, l_min = _diff_attend(q_groups, segments, lam, gain, d_head, key_chunk, ksq_ref[...])
        write(outs)

        @pl.when(jnp.logical_not(l_min >= BOUND_OK))
        def _():
            write(_diff_attend(q_groups, segments, lam, gain, d_head, key_chunk)[0])

    n = qt_ref.shape[1] // chains
    groups = [slice(g * n, (g + 1) * n) for g in range(chains)]

    def write_latent(outs):
        for g, o in zip(groups, outs):
            o_ref[g, :] = o.astype(o_ref.dtype)

    attend([qt_ref[:, g] for g in groups], [(kc_ref, vtc_ref), (kl_ref, vtl_ref)], write_latent)

    if with_ctx_queries:
        def write_ctx(outs):
            oc_ref[...] = outs[0].astype(oc_ref.dtype)

        @pl.when(first_step)
        def _():
            attend([qtc_ref[...]], [(kc_ref, vtc_ref)], write_ctx)


def _attn_call(qt, kk, vt, lam_vecs, subln_g, lam_init, batch, seq, ctx_len, n_heads, d_head,
               with_ctx_queries, tq, key_chunk=ATTN_KEY_CHUNK, chains=None):
    dv = 2 * d_head
    assert dv == LANES
    t_lat = batch * seq
    n_lat = seq // tq
    ctx_blk = t_lat // ctx_len
    if chains is None:
        chains = min(ATTN_CHAINS, tq // LANES)

    full = lambda shape: pl.BlockSpec(shape, lambda b, h, qi: (0, 0))
    in_specs = [full((4, LANES)), full((1, dv)),
                pl.BlockSpec((dv, tq), lambda b, h, qi: (h, b * n_lat + qi)),
                pl.BlockSpec((ctx_len, dv), lambda b, h, qi: (ctx_blk + b, h)),
                pl.BlockSpec((dv, ctx_len), lambda b, h, qi: (h, ctx_blk + b)),
                pl.BlockSpec((seq, dv), lambda b, h, qi: (b, h)),
                pl.BlockSpec((dv, seq), lambda b, h, qi: (h, b))]
    args = [lam_vecs, subln_g.reshape(1, dv), qt, kk, vt, kk, vt]
    out_specs = [pl.BlockSpec((tq, dv), lambda b, h, qi: (b * n_lat + qi, h))]
    out_shape = [jax.ShapeDtypeStruct((t_lat, n_heads * dv), BF16)]
    if with_ctx_queries:
        in_specs.append(pl.BlockSpec((dv, ctx_len), lambda b, h, qi: (h, ctx_blk + b)))
        args.append(qt)
        out_specs.append(pl.BlockSpec((ctx_len, dv), lambda b, h, qi: (b, h)))
        out_shape.append(jax.ShapeDtypeStruct((batch * ctx_len, n_heads * dv), BF16))
    out = pl.pallas_call(
        functools.partial(_attn_kernel, lam_init=lam_init, d_head=d_head, key_chunk=key_chunk,
                          with_ctx_queries=with_ctx_queries, chains=chains),
        grid=(batch, n_heads, n_lat),
        in_specs=in_specs,
        out_specs=out_specs,
        out_shape=out_shape,
        scratch_shapes=[pltpu.VMEM((1, LANES), F32)],
        compiler_params=_params("parallel", "parallel", "arbitrary"),
        name="diff_attention",
    )(*args)
    return (out[0], out[1]) if with_ctx_queries else (out[0], None)


def _conv_kernel(prev_ref, cur_ref, next_ref, dw_ref, dwb_ref, g_ref, b_ref, o_ref,
                 xs_ref, sh_ref, y_ref, *, rows_tile, width, seq, ctx_len, t_lat):
    i = pl.program_id(0)
    r0 = i * rows_tile
    is_lat = r0 < t_lat
    per_seq = jnp.where(is_lat, seq // rows_tile, ctx_len // rows_tile)
    pos = jnp.where(is_lat, i % (seq // rows_tile), (i - t_lat // rows_tile) % (ctx_len // rows_tile))
    keep_prev = (pos > 0).astype(F32)
    keep_next = (pos < per_seq - 1).astype(F32)
    h = CONV_HALO
    xs_ref[0:h, :] = jnp.where(keep_prev > 0, prev_ref[...], 0.0)
    xs_ref[h:h + rows_tile, :] = cur_ref[...]
    xs_ref[h + rows_tile:2 * h + rows_tile, :] = jnp.where(keep_next > 0, next_ref[...], 0.0)

    base = h - width // 2
    n_ch = cur_ref.shape[1]
    sh_rows = sh_ref.shape[1]
    for s in range(1, SUBLANES):
        sh_ref[s - 1] = xs_ref[s:s + sh_rows, :]

    def lane_chunk(c, carry):
        cols = pl.ds(pl.multiple_of(c * LANES, LANES), LANES)
        taps = [jnp.broadcast_to(dw_ref[t:t + 1, cols], (SUBLANES, LANES)) for t in range(width)]
        bias = jnp.broadcast_to(dwb_ref[:, cols], (SUBLANES, LANES))
        for row in range(0, rows_tile, SUBLANES):
            acc = bias
            for t in range(width):
                s, r0 = (base + t) % SUBLANES, row + (base + t) // SUBLANES * SUBLANES
                src = xs_ref[r0:r0 + SUBLANES, cols] if s == 0 else sh_ref[s - 1, r0:r0 + SUBLANES, cols]
                acc = acc + src * taps[t]
            y_ref[row:row + SUBLANES, cols] = acc
        return carry

    lax.fori_loop(0, n_ch // LANES, lane_chunk, 0)

    y = _layer_norm_rows(y_ref[...]) * g_ref[...] + b_ref[...]
    o_ref[...] = (y * _sigmoid(y)).astype(o_ref.dtype)


def _conv_call(u, dw, dw_b, ln_g, ln_b, rows, seq, ctx_len, t_lat, tile):
    width, n_ch = dw.shape
    assert width // 2 < CONV_HALO and tile % CONV_HALO == 0
    hb = tile // CONV_HALO
    n_halo_blocks = rows // CONV_HALO
    dw_pad = jnp.zeros((2 * CONV_HALO, n_ch), F32).at[:width].set(dw)
    vec = pl.BlockSpec((1, n_ch), lambda i: (0, 0))
    sh_rows = tile + (CONV_HALO - width // 2 + width - 1) // SUBLANES * SUBLANES
    assert SUBLANES - 1 + sh_rows <= tile + 2 * CONV_HALO
    return pl.pallas_call(
        functools.partial(_conv_kernel, rows_tile=tile, width=width, seq=seq, ctx_len=ctx_len,
                          t_lat=t_lat),
        grid=(rows // tile,),
        in_specs=[pl.BlockSpec((CONV_HALO, n_ch), lambda i: (jnp.maximum(i * hb - 1, 0), 0)),
                  pl.BlockSpec((tile, n_ch), lambda i: (i, 0)),
                  pl.BlockSpec((CONV_HALO, n_ch),
                               lambda i: (jnp.minimum((i + 1) * hb, n_halo_blocks - 1), 0)),
                  pl.BlockSpec((2 * CONV_HALO, n_ch), lambda i: (0, 0)), vec, vec, vec],
        out_specs=pl.BlockSpec((tile, n_ch), lambda i: (i, 0)),
        out_shape=jax.ShapeDtypeStruct((rows, n_ch), BF16),
        scratch_shapes=[pltpu.VMEM((tile + 2 * CONV_HALO, n_ch), F32),
                        pltpu.VMEM((SUBLANES - 1, sh_rows, n_ch), F32),
                        pltpu.VMEM((tile, n_ch), F32)],
        compiler_params=_params("parallel"),
        name="conformer_conv",
    )(u, u, u, dw_pad, dw_b.reshape(1, n_ch), ln_g.reshape(1, n_ch), ln_b.reshape(1, n_ch))


def kernel(x, c, ctx, c_ctx, w_mod, b_mod, ln_g, ln_b, ffn1_w_gate, ffn1_w_up, ffn1_w_down,
           w_in, b_glu, conv_dw, conv_dw_b, conv_ln_g, conv_ln_b, lambda_q1, lambda_k1,
           lambda_q2, lambda_k2, attn_subln_g, w_out, ffn2_w_gate, ffn2_w_up, ffn2_w_down):
    batch, seq, d = x.shape
    ctx_len = ctx.shape[1]
    depth = w_mod.shape[0]
    d_ff = ffn1_w_gate.shape[2]
    conv_ch = conv_dw.shape[2]
    d_head = lambda_q1.shape[1]
    dv = attn_subln_g.shape[1]
    attn_w = w_out.shape[1] - conv_ch
    n_heads = attn_w // dv
    t_lat, t_ctx = batch * seq, batch * ctx_len
    t_all = t_lat + t_ctx
    alpha = (2 * depth) ** 0.25
    assert batch + 1 <= MOD_ROWS and dv == 2 * d_head and t_lat % ctx_len == 0

    t = _tile_plan(seq, ctx_len, t_ctx, d, d_ff, conv_ch, attn_w)

    c_rows = jnp.zeros((MOD_ROWS, d), F32).at[:batch].set(c).at[batch].set(c_ctx)
    mods = _mod_call(c_rows, w_mod, b_mod).reshape(depth, MOD_ROWS * N_MOD, 1, d)

    xr, xm = _entry_call(x.reshape(t_lat, d), ctx.reshape(t_ctx, d), mods[0], seq, batch, t.norm_rows)

    tab = _rope_tables(seq, t.rows, d_head)

    for i in range(depth):
        last = i == depth - 1
        lam_init = 0.8 - 0.6 * math.exp(-0.3 * i)
        mod = mods[i]
        norm = functools.partial(_postnorm_call, seq=seq, batch=batch, alpha=alpha, tm=t.norm_rows)

        hid = _dual_call(xm, ffn1_w_gate, 0, ffn1_w_up, 0, i, d_ff, None, "swiglu", BF16, t_all,
                         t.rows, t.ffn_cols)
        y = _matmul_call([hid], ffn1_w_down, i, d, BF16, t_all, t.rows, t.down_cols)
        xr, xm = norm(xr, y, mod, 2, 0.5, ln_g[i, 0], ln_b[i, 0], mod, (3, 4), t_all)

        rows = t_lat if last else t_all
        u = _dual_call(xm, w_in, 0, w_in, conv_ch, i, conv_ch, b_glu[i].reshape(1, -1), "glu",
                       F32, rows, t.rows, t.glu_cols)
        head_proj = functools.partial(_head_proj_call, xm, w_in, i, n_out=attn_w, seq=seq,
                                      t_lat=t_lat, d_head=d_head, tm=t.rows, tn=t.head_cols)
        qt = head_proj(col_off=2 * conv_ch, tab=tab, scale=d_head ** -0.5 * LOG2_E,
                       transpose=True, rows=rows)
        kk = head_proj(col_off=2 * conv_ch + attn_w, tab=tab, scale=1.0, transpose=False,
                       rows=t_all)
        vt = head_proj(col_off=2 * conv_ch + 2 * attn_w, tab=None, scale=1.0, transpose=True,
                       rows=t_all)
        lam_vecs = jnp.zeros((4, LANES), F32).at[:, :d_head].set(
            jnp.stack([lambda_q1[i], lambda_k1[i], lambda_q2[i], lambda_k2[i]]).astype(F32))
        attn, attn_ctx = _attn_call(qt, kk, vt, lam_vecs, attn_subln_g[i], lam_init, batch, seq,
                                    ctx_len, n_heads, d_head, not last, t.queries)
        if not last:
            attn = jnp.concatenate([attn, attn_ctx], axis=0)
        conv = _conv_call(u, conv_dw[i], conv_dw_b[i], conv_ln_g[i], conv_ln_b[i], rows, seq,
                          ctx_len, t_lat, t.conv_rows)
        y = _matmul_call([conv, attn], w_out, i, d, BF16, rows, t.rows, t.out_cols)
        xr, xm = norm(xr, y, mod, 5, 1.0, ln_g[i, 1], ln_b[i, 1], mod, (6, 7), rows)

        hid = _dual_call(xm, ffn2_w_gate, 0, ffn2_w_up, 0, i, d_ff, None, "swiglu", BF16, rows,
                         t.rows, t.ffn_cols)
        y = _matmul_call([hid], ffn2_w_down, i, d, BF16, rows, t.rows, t.down_cols)
        nxt = None if last else mods[i + 1]
        xr, xm = norm(xr, y, mod, 8, 0.5, ln_g[i, 2], ln_b[i, 2], nxt, (0, 1), rows)

    return xr.reshape(batch, seq, d)
```

```python
import functools
import math
from typing import NamedTuple

import jax
import jax.numpy as jnp
import numpy as np
from jax import lax
from jax.experimental import pallas as pl
from jax.experimental.pallas import tpu as pltpu

GRID_W = 64
ROPE_BASE = 10000.0
LN_EPS = 1e-5
N_MOD = 9
LANES = 128
SUBLANES = 8
BF16_ROWS = 16
MOD_ROWS = 8
ATTN_KEY_CHUNK = 2048
ATTN_CHAINS = 8
LOG2_E = 1.4426950408889634
CONV_HALO = 16
VMEM_LIMIT_BYTES = 56 * 1024 * 1024

BF16 = jnp.bfloat16
F32 = jnp.float32


def _pick(n, pref):
    t = min(n, pref)
    while n % t:
        t //= 2
    return t


class _Tiles(NamedTuple):
    rows: int
    norm_rows: int
    conv_rows: int
    queries: int
    ffn_cols: int
    glu_cols: int
    down_cols: int
    out_cols: int
    head_cols: int


def _tile_plan(seq, ctx_len, t_ctx, d, d_ff, conv_ch, attn_w):
    unit = math.gcd(seq, t_ctx)
    return _Tiles(rows=_pick(unit, 1024), norm_rows=_pick(unit, 256),
                  conv_rows=_pick(math.gcd(seq, ctx_len), 256), queries=_pick(seq, 1024),
                  ffn_cols=_pick(d_ff, 512), glu_cols=_pick(conv_ch, 512),
                  down_cols=_pick(d, 512), out_cols=_pick(d, 1024),
                  head_cols=_pick(attn_w, 1024))


def _params(*sem):
    return pltpu.CompilerParams(dimension_semantics=sem, vmem_limit_bytes=VMEM_LIMIT_BYTES)


def _sigmoid(v):
    return 1.0 / (1.0 + jnp.exp(-v))


def _group_of(row0, seq, batch):
    return jnp.minimum(row0 // seq, batch)


def _mod_spec(d, tm, seq, batch, slot):
    return pl.BlockSpec((1, 1, d), lambda i: (_group_of(i * tm, seq, batch) * N_MOD + slot, 0, 0))


def _mod_kernel(c_ref, w_ref, b_ref, o_ref):
    c = c_ref[...]
    s = (c * _sigmoid(c)).astype(BF16)
    w = w_ref[0].astype(BF16)
    o_ref[0] = jnp.dot(s, w, preferred_element_type=F32) + b_ref[0]


def _mod_call(c_rows, w_mod, b_mod):
    depth, d, n = w_mod.shape
    tn = _pick(n, 512)
    return pl.pallas_call(
        _mod_kernel,
        grid=(depth, n // tn),
        in_specs=[pl.BlockSpec((MOD_ROWS, d), lambda l, j: (0, 0)),
                  pl.BlockSpec((1, d, tn), lambda l, j: (l, 0, j)),
                  pl.BlockSpec((1, 1, tn), lambda l, j: (l, 0, j))],
        out_specs=pl.BlockSpec((1, MOD_ROWS, tn), lambda l, j: (l, 0, j)),
        out_shape=jax.ShapeDtypeStruct((depth, MOD_ROWS, n), F32),
        compiler_params=_params("parallel", "parallel"),
        name="adaln_mod",
    )(c_rows, w_mod, b_mod.reshape(depth, 1, n))


def _layer_norm_rows(z):
    mu = jnp.mean(z, axis=-1, keepdims=True)
    zc = z - mu
    var = jnp.mean(zc * zc, axis=-1, keepdims=True)
    return zc * lax.rsqrt(var + LN_EPS)


def _entry_kernel(x_ref, c_ref, sh_ref, sc_ref, xn_ref, xm_ref, *, n_lat_tiles):
    def emit(src_ref):
        y = _layer_norm_rows(src_ref[...])
        xn_ref[...] = y
        xm_ref[...] = (y * (1.0 + sc_ref[0]) + sh_ref[0]).astype(BF16)

    i = pl.program_id(0)
    pl.when(i < n_lat_tiles)(lambda: emit(x_ref))
    pl.when(i >= n_lat_tiles)(lambda: emit(c_ref))


def _entry_call(x2, ctx2, mod, seq, batch, tm):
    (t_lat, d), t_ctx = x2.shape, ctx2.shape[0]
    n_lat = t_lat // tm
    t = t_lat + t_ctx
    row = pl.BlockSpec((tm, d), lambda i: (i, 0))
    return pl.pallas_call(
        functools.partial(_entry_kernel, n_lat_tiles=n_lat),
        grid=(t // tm,),
        in_specs=[pl.BlockSpec((tm, d), lambda i: (jnp.minimum(i, n_lat - 1), 0)),
                  pl.BlockSpec((tm, d), lambda i: (jnp.maximum(i - n_lat, 0), 0)),
                  _mod_spec(d, tm, seq, batch, 0), _mod_spec(d, tm, seq, batch, 1)],
        out_specs=[row, row],
        out_shape=[jax.ShapeDtypeStruct((t, d), F32), jax.ShapeDtypeStruct((t, d), BF16)],
        compiler_params=_params("parallel"),
        name="entry_norm",
    )(x2, ctx2, mod, mod)


def _postnorm_kernel(*refs, alpha, coef, with_next):
    if with_next:
        x_ref, y_ref, gate_ref, g_ref, b_ref, sh_ref, sc_ref, xn_ref, xm_ref = refs
    else:
        x_ref, y_ref, gate_ref, g_ref, b_ref, xn_ref = refs
    z = alpha * x_ref[...] + (coef * gate_ref[0]) * y_ref[...]
    y = _layer_norm_rows(z) * g_ref[...] + b_ref[...]
    xn_ref[...] = y
    if with_next:
        xm_ref[...] = (y * (1.0 + sc_ref[0]) + sh_ref[0]).astype(BF16)


def _postnorm_call(x, y, mod_gate, gate_slot, coef, g, b, mod_next, next_slots,
                   rows, seq, batch, alpha, tm):
    d = x.shape[1]
    row = pl.BlockSpec((tm, d), lambda i: (i, 0))
    vec = pl.BlockSpec((1, d), lambda i: (0, 0))
    with_next = mod_next is not None
    in_specs = [row, row, _mod_spec(d, tm, seq, batch, gate_slot), vec, vec]
    args = [x, y, mod_gate, g.reshape(1, d), b.reshape(1, d)]
    out_specs = [row]
    out_shape = [jax.ShapeDtypeStruct((rows, d), F32)]
    if with_next:
        in_specs += [_mod_spec(d, tm, seq, batch, next_slots[0]),
                     _mod_spec(d, tm, seq, batch, next_slots[1])]
        args += [mod_next, mod_next]
        out_specs.append(row)
        out_shape.append(jax.ShapeDtypeStruct((rows, d), BF16))
    out = pl.pallas_call(
        functools.partial(_postnorm_kernel, alpha=alpha, coef=coef, with_next=with_next),
        grid=(rows // tm,),
        in_specs=in_specs,
        out_specs=out_specs,
        out_shape=out_shape,
        compiler_params=_params("parallel"),
        name="post_norm",
    )(*args)
    return (out[0], out[1]) if with_next else (out[0], None)


def _stream_chunks(k, n_row_tiles):
    n = 1
    while n * 2 <= n_row_tiles and k % (n * 2 * BF16_ROWS) == 0:
        n *= 2
    return n


def _streamed_weight_spec(k, tn, n_chunks, n_sweeps, layer, row_off, col_off):
    kb = k // n_chunks
    assert row_off % kb == 0 and col_off % tn == 0

    def idx(s, i):
        return (layer, row_off // kb + jnp.minimum(i, n_chunks - 1),
                col_off // tn + jnp.minimum(s, n_sweeps - 1))
    return pl.BlockSpec((None, kb, tn), idx)


def _stage_weights(w_refs, wb_refs, n_chunks, n_sweeps):
    s, i = pl.program_id(0), pl.program_id(1)

    @pl.when(jnp.logical_and(i < n_chunks, s < n_sweeps))
    def _():
        for w_ref, wb_ref in zip(w_refs, wb_refs):
            kb = w_ref.shape[0]
            wb_ref[s % 2, pl.ds(pl.multiple_of(i * kb, kb), kb), :] = w_ref[...].astype(BF16)


def _row_tile(s, i):
    return jnp.where(s == 0, 0, i)


def _col_block(s):
    return jnp.maximum(s - 1, 0)


def _dual_kernel(*refs, act, with_bias, n_chunks, n_sweeps):
    if with_bias:
        a_ref, w1_ref, w2_ref, b1_ref, b2_ref, o_ref, w1b_ref, w2b_ref = refs
    else:
        a_ref, w1_ref, w2_ref, o_ref, w1b_ref, w2b_ref = refs
    _stage_weights([w1_ref, w2_ref], [w1b_ref, w2b_ref], n_chunks, n_sweeps)
    sweep = pl.program_id(0)
    slot = (sweep + 1) % 2

    @pl.when(sweep > 0)
    def _():
        a = a_ref[...]
        p1 = jnp.dot(a, w1b_ref[slot], preferred_element_type=F32)
        p2 = jnp.dot(a, w2b_ref[slot], preferred_element_type=F32)
        if with_bias:
            p1 = p1 + b1_ref[...]
            p2 = p2 + b2_ref[...]
        if act == "swiglu":
            r = (p1 * _sigmoid(p1)) * p2
        else:
            r = p1 * _sigmoid(p2)
        o_ref[...] = r.astype(o_ref.dtype)


def _dual_call(a, w1, off1, w2, off2, layer, n_out, bias, act, out_dtype, rows, tm, tn):
    k = a.shape[1]
    with_bias = bias is not None
    n_sweeps, n_tiles = n_out // tn, rows // tm
    n_chunks = _stream_chunks(k, n_tiles)
    in_specs = [pl.BlockSpec((tm, k), lambda s, i: (_row_tile(s, i), 0)),
                _streamed_weight_spec(k, tn, n_chunks, n_sweeps, layer, 0, off1),
                _streamed_weight_spec(k, tn, n_chunks, n_sweeps, layer, 0, off2)]
    args = [a, w1, w2]
    if with_bias:
        in_specs += [pl.BlockSpec((1, tn), lambda s, i: (0, _col_block(s) + off1 // tn)),
                     pl.BlockSpec((1, tn), lambda s, i: (0, _col_block(s) + off2 // tn))]
        args += [bias, bias]
    return pl.pallas_call(
        functools.partial(_dual_kernel, act=act, with_bias=with_bias, n_chunks=n_chunks,
                          n_sweeps=n_sweeps),
        grid=(n_sweeps + 1, n_tiles),
        in_specs=in_specs,
        out_specs=pl.BlockSpec((tm, tn), lambda s, i: (_row_tile(s, i), _col_block(s))),
        out_shape=jax.ShapeDtypeStruct((rows, n_out), out_dtype),
        scratch_shapes=[pltpu.VMEM((2, k, tn), BF16), pltpu.VMEM((2, k, tn), BF16)],
        compiler_params=_params("arbitrary", "arbitrary"),
        name="gated_proj_" + act,
    )(*args)


def _matmul_kernel(*refs, n_pairs, n_chunks, n_sweeps):
    a_refs, w_refs = refs[:n_pairs], refs[n_pairs:2 * n_pairs]
    o_ref, wb_refs = refs[2 * n_pairs], refs[2 * n_pairs + 1:]
    _stage_weights(w_refs, wb_refs, n_chunks, n_sweeps)
    sweep = pl.program_id(0)
    slot = (sweep + 1) % 2

    @pl.when(sweep > 0)
    def _():
        acc = jnp.dot(a_refs[0][...], wb_refs[0][slot], preferred_element_type=F32)
        for a_ref, wb_ref in zip(a_refs[1:], wb_refs[1:]):
            acc = acc + jnp.dot(a_ref[...], wb_ref[slot], preferred_element_type=F32)
        o_ref[...] = acc.astype(o_ref.dtype)


def _matmul_call(a_list, w, layer, n_out, out_dtype, rows, tm, tn):
    n_pairs = len(a_list)
    n_sweeps, n_tiles = n_out // tn, rows // tm
    n_chunks = min(_stream_chunks(a.shape[1], n_tiles) for a in a_list)
    in_specs, w_specs, scratch, row_off = [], [], [], 0
    for a in a_list:
        kp = a.shape[1]
        in_specs.append(pl.BlockSpec((tm, kp), lambda s, i: (_row_tile(s, i), 0)))
        w_specs.append(_streamed_weight_spec(kp, tn, n_chunks, n_sweeps, layer, row_off, 0))
        scratch.append(pltpu.VMEM((2, kp, tn), BF16))
        row_off += kp
    return pl.pallas_call(
        functools.partial(_matmul_kernel, n_pairs=n_pairs, n_chunks=n_chunks, n_sweeps=n_sweeps),
        grid=(n_sweeps + 1, n_tiles),
        in_specs=in_specs + w_specs,
        out_specs=pl.BlockSpec((tm, tn), lambda s, i: (_row_tile(s, i), _col_block(s))),
        out_shape=jax.ShapeDtypeStruct((rows, n_out), out_dtype),
        scratch_shapes=scratch,
        compiler_params=_params("arbitrary", "arbitrary"),
        name="proj",
    )(*a_list, *([w] * n_pairs))


def _rope_tables(seq, tm, d_head):
    axis_dim = d_head // 2
    half = axis_dim // 2
    inv = (1.0 / (ROPE_BASE ** (np.arange(half, dtype=np.float32) / half))).astype(np.float32)
    pos = np.arange(seq)
    r = (pos // GRID_W).astype(np.float32)
    col = (pos % GRID_W).astype(np.float32)
    ang_r = r[:, None] * inv[None, :]
    ang_c = col[:, None] * inv[None, :]
    ang = np.concatenate([ang_r, ang_r, ang_c, ang_c], axis=-1)
    ang = np.tile(ang, (1, LANES // d_head))
    first_half = (np.arange(LANES) % axis_dim) < half
    cos, sin = np.cos(ang), np.sin(ang)
    sin_up = np.where(first_half[None, :], -sin, 0.0)
    sin_dn = np.where(first_half[None, :], 0.0, sin)
    ident = np.stack([np.ones((tm, LANES)), np.zeros((tm, LANES)), np.zeros((tm, LANES))])
    tab = np.concatenate([np.stack([cos, sin_up, sin_dn]), ident], axis=1)
    return jnp.asarray(tab.astype(np.float32))


def _head_proj_kernel(*refs, rope, half, scale, transpose, n_chunks, n_sweeps):
    if rope:
        a_ref, w_ref, tab_ref, o_ref, wb_ref = refs
    else:
        a_ref, w_ref, o_ref, wb_ref = refs
    _stage_weights([w_ref], [wb_ref], n_chunks, n_sweeps)
    sweep = pl.program_id(0)
    slot = (sweep + 1) % 2

    @pl.when(sweep > 0)
    def _():
        acc = jnp.dot(a_ref[...], wb_ref[slot], preferred_element_type=F32)
        for c in range(acc.shape[1] // LANES):
            cols = slice(c * LANES, (c + 1) * LANES)
            r = acc[:, cols]
            if rope:
                up = pltpu.roll(r, LANES - half, 1)
                dn = pltpu.roll(r, half, 1)
                r = r * tab_ref[0] + up * tab_ref[1] + dn * tab_ref[2]
            if scale != 1.0:
                r = r * scale
            if transpose:
                o_ref[cols, :] = r.T.astype(o_ref.dtype)
            else:
                o_ref[:, cols] = r.astype(o_ref.dtype)


def _head_proj_call(a, w, layer, col_off, n_out, tab, seq, t_lat, d_head, scale, transpose, rows,
                    tm, tn):
    k = a.shape[1]
    rope = tab is not None
    n_sweeps, n_tiles = n_out // tn, rows // tm
    n_chunks = _stream_chunks(k, n_tiles)

    def tab_idx(s, i):
        r0 = _row_tile(s, i) * tm
        return (0, jnp.where(r0 < t_lat, (r0 % seq) // tm, seq // tm), 0)

    in_specs = [pl.BlockSpec((tm, k), lambda s, i: (_row_tile(s, i), 0)),
                _streamed_weight_spec(k, tn, n_chunks, n_sweeps, layer, 0, col_off)]
    args = [a, w]
    if rope:
        in_specs.append(pl.BlockSpec((3, tm, LANES), tab_idx))
        args.append(tab)
    if transpose:
        out_spec = pl.BlockSpec((tn, tm), lambda s, i: (_col_block(s), _row_tile(s, i)))
        out_shape = jax.ShapeDtypeStruct((n_out, rows), BF16)
    else:
        out_spec = pl.BlockSpec((tm, tn), lambda s, i: (_row_tile(s, i), _col_block(s)))
        out_shape = jax.ShapeDtypeStruct((rows, n_out), BF16)
    return pl.pallas_call(
        functools.partial(_head_proj_kernel, rope=rope, half=d_head // 4, scale=scale,
                          transpose=transpose, n_chunks=n_chunks, n_sweeps=n_sweeps),
        grid=(n_sweeps + 1, n_tiles),
        in_specs=in_specs,
        out_specs=out_spec,
        out_shape=out_shape,
        scratch_shapes=[pltpu.VMEM((2, k, tn), BF16)],
        compiler_params=_params("arbitrary", "arbitrary"),
        name="head_proj",
    )(*args)


NEG_BIG = -1e30
BOUND_MARGIN = 1.01
BOUND_OK = 2.0 ** -80


def _max_key_sqnorm(k_refs, d_head):
    d = lax.broadcasted_iota(jnp.int32, (2 * d_head, LANES), 0)
    m = lax.broadcasted_iota(jnp.int32, (2 * d_head, LANES), 1)
    pick = jnp.where(d // d_head == m, 1.0, 0.0).astype(BF16)
    best = jnp.zeros((1, LANES), F32)
    for k_ref in k_refs:
        k = k_ref[...]
        sq = jnp.dot(k * k, pick, preferred_element_type=F32)
        best = jnp.maximum(best, jnp.max(sq, axis=0, keepdims=True))
    return best


def _diff_attend(qts, segments, lam, gain, d_head, key_chunk, key_sqnorm=None):
    bounded = key_sqnorm is not None
    chunks = [(k_ref, vt_ref, slice(c0, min(c0 + key_chunk, k_ref.shape[0])))
              for k_ref, vt_ref in segments for c0 in range(0, k_ref.shape[0], key_chunk)]

    def scores(qmt, chunk):
        return jnp.dot(chunk[0][chunk[2], :], qmt, preferred_element_type=F32)

    state = []
    for qt in qts:
        n = qt.shape[1]
        row = lax.broadcasted_iota(jnp.int32, qt.shape, 0)
        zero = jnp.zeros_like(qt)
        qmt = jnp.concatenate([jnp.where(row < d_head, qt, zero),
                               jnp.where(row >= d_head, qt, zero)], axis=1)
        if bounded:
            q2 = qmt.astype(F32)
            q2 = jnp.sum(q2 * q2, axis=0, keepdims=True)
            k2 = jnp.concatenate([jnp.broadcast_to(key_sqnorm[:, 0:1], (1, n)),
                                  jnp.broadcast_to(key_sqnorm[:, 1:2], (1, n))], axis=1)
            m = jnp.sqrt(q2 * k2) * BOUND_MARGIN
        else:
            m = jnp.full((1, 2 * n), NEG_BIG, F32)
        state.append(dict(qmt=qmt, m=m, l=jnp.zeros((1, 2 * n), F32),
                          acc=jnp.zeros((qt.shape[0], 2 * n), F32),
                          s_next=scores(qmt, chunks[0])))
    for i, (k_ref, vt_ref, keys) in enumerate(chunks):
        for st in state:
            s = st["s_next"]
            st["s_next"] = scores(st["qmt"], chunks[i + 1]) if i + 1 < len(chunks) else None
            if bounded:
                p = jnp.exp2(s - st["m"])
                st["l"] = st["l"] + jnp.sum(p, axis=0, keepdims=True)
                st["acc"] = st["acc"] + jnp.dot(vt_ref[:, keys], p.astype(BF16),
                                                preferred_element_type=F32)
            else:
                m_new = jnp.maximum(st["m"], jnp.max(s, axis=0, keepdims=True))
                alpha = jnp.exp2(st["m"] - m_new)
                p = jnp.exp2(s - m_new)
                st["l"] = st["l"] * alpha + jnp.sum(p, axis=0, keepdims=True)
                st["acc"] = st["acc"] * alpha + jnp.dot(vt_ref[:, keys], p.astype(BF16),
                                                        preferred_element_type=F32)
                st["m"] = m_new
    outs = []
    for st in state:
        n = st["qmt"].shape[1] // 2
        o = st["acc"] * (1.0 / st["l"])
        d = (o[:, :n] - lam * o[:, n:]).T
        outs.append(d * lax.rsqrt(jnp.mean(d * d, axis=-1, keepdims=True) + LN_EPS) * gain)
    return outs, functools.reduce(jnp.minimum, [jnp.min(st["l"]) for st in state])


def _attn_kernel(*refs, lam_init, d_head, key_chunk, with_ctx_queries, chains):
    if with_ctx_queries:
        (lam_ref, g_ref, qt_ref, kc_ref, vtc_ref, kl_ref, vtl_ref, qtc_ref, o_ref, oc_ref,
         ksq_ref) = refs
    else:
        lam_ref, g_ref, qt_ref, kc_ref, vtc_ref, kl_ref, vtl_ref, o_ref, ksq_ref = refs
    lv = lam_ref[...]
    lam = (jnp.exp(jnp.sum(lv[0:1] * lv[1:2], axis=-1, keepdims=True))
           - jnp.exp(jnp.sum(lv[2:3] * lv[3:4], axis=-1, keepdims=True)) + lam_init)
    gain = g_ref[...] * (1.0 - lam_init)
    first_step = pl.program_id(2) == 0

    @pl.when(first_step)
    def _():
        ksq_ref[...] = _max_key_sqnorm([kc_ref, kl_ref], d_head)

    def attend(q_groups, segments, write):
        outs, l_min = _diff_attend(q_groups, segments, lam, gain, d_head, key_chunk, ksq_ref[...])
        write(outs)

        @pl.when(jnp.logical_not(l_min >= BOUND_OK))
        def _():
            write(_diff_attend(q_groups, segments, lam, gain, d_head, key_chunk)[0])

    n = qt_ref.shape[1] // chains
    groups = [slice(g * n, (g + 1) * n) for g in range(chains)]

    def write_latent(outs):
        for g, o in zip(groups, outs):
            o_ref[g, :] = o.astype(o_ref.dtype)

    attend([qt_ref[:, g] for g in groups], [(kc_ref, vtc_ref), (kl_ref, vtl_ref)], write_latent)

    if with_ctx_queries:
        def write_ctx(outs):
            oc_ref[...] = outs[0].astype(oc_ref.dtype)

        @pl.when(first_step)
        def _():
            attend([qtc_ref[...]], [(kc_ref, vtc_ref)], write_ctx)


def _attn_call(qt, kk, vt, lam_vecs, subln_g, lam_init, batch, seq, ctx_len, n_heads, d_head,
               with_ctx_queries, tq, key_chunk=ATTN_KEY_CHUNK, chains=None):
    dv = 2 * d_head
    assert dv == LANES
    t_lat = batch * seq
    n_lat = seq // tq
    ctx_blk = t_lat // ctx_len
    if chains is None:
        chains = min(ATTN_CHAINS, tq // LANES)

    full = lambda shape: pl.BlockSpec(shape, lambda b, h, qi: (0, 0))
    in_specs = [full((4, LANES)), full((1, dv)),
                pl.BlockSpec((dv, tq), lambda b, h, qi: (h, b * n_lat + qi)),
                pl.BlockSpec((ctx_len, dv), lambda b, h, qi: (ctx_blk + b, h)),
                pl.BlockSpec((dv, ctx_len), lambda b, h, qi: (h, ctx_blk + b)),
                pl.BlockSpec((seq, dv), lambda b, h, qi: (b, h)),
                pl.BlockSpec((dv, seq), lambda b, h, qi: (h, b))]
    args = [lam_vecs, subln_g.reshape(1, dv), qt, kk, vt, kk, vt]
    out_specs = [pl.BlockSpec((tq, dv), lambda b, h, qi: (b * n_lat + qi, h))]
    out_shape = [jax.ShapeDtypeStruct((t_lat, n_heads * dv), BF16)]
    if with_ctx_queries:
        in_specs.append(pl.BlockSpec((dv, ctx_len), lambda b, h, qi: (h, ctx_blk + b)))
        args.append(qt)
        out_specs.append(pl.BlockSpec((ctx_len, dv), lambda b, h, qi: (b, h)))
        out_shape.append(jax.ShapeDtypeStruct((batch * ctx_len, n_heads * dv), BF16))
    out = pl.pallas_call(
        functools.partial(_attn_kernel, lam_init=lam_init, d_head=d_head, key_chunk=key_chunk,
                          with_ctx_queries=with_ctx_queries, chains=chains),
        grid=(batch, n_heads, n_lat),
        in_specs=in_specs,
        out_specs=out_specs,
        out_shape=out_shape,
        scratch_shapes=[pltpu.VMEM((1, LANES), F32)],
        compiler_params=_params("parallel", "parallel", "arbitrary"),
        name="diff_attention",
    )(*args)
    return (out[0], out[1]) if with_ctx_queries else (out[0], None)


def _conv_kernel(prev_ref, cur_ref, next_ref, dw_ref, dwb_ref, g_ref, b_ref, o_ref,
                 xs_ref, sh_ref, y_ref, *, rows_tile, width, seq, ctx_len, t_lat):
    i = pl.program_id(0)
    r0 = i * rows_tile
    is_lat = r0 < t_lat
    per_seq = jnp.where(is_lat, seq // rows_tile, ctx_len // rows_tile)
    pos = jnp.where(is_lat, i % (seq // rows_tile), (i - t_lat // rows_tile) % (ctx_len // rows_tile))
    keep_prev = (pos > 0).astype(F32)
    keep_next = (pos < per_seq - 1).astype(F32)
    h = CONV_HALO
    xs_ref[0:h, :] = jnp.where(keep_prev > 0, prev_ref[...], 0.0)
    xs_ref[h:h + rows_tile, :] = cur_ref[...]
    xs_ref[h + rows_tile:2 * h + rows_tile, :] = jnp.where(keep_next > 0, next_ref[...], 0.0)

    base = h - width // 2
    n_ch = cur_ref.shape[1]
    sh_rows = sh_ref.shape[1]
    for s in range(1, SUBLANES):
        sh_ref[s - 1] = xs_ref[s:s + sh_rows, :]

    def lane_chunk(c, carry):
        cols = pl.ds(pl.multiple_of(c * LANES, LANES), LANES)
        taps = [jnp.broadcast_to(dw_ref[t:t + 1, cols], (SUBLANES, LANES)) for t in range(width)]
        bias = jnp.broadcast_to(dwb_ref[:, cols], (SUBLANES, LANES))
        for row in range(0, rows_tile, SUBLANES):
            acc = bias
            for t in range(width):
                s, r0 = (base + t) % SUBLANES, row + (base + t) // SUBLANES * SUBLANES
                src = xs_ref[r0:r0 + SUBLANES, cols] if s == 0 else sh_ref[s - 1, r0:r0 + SUBLANES, cols]
                acc = acc + src * taps[t]
            y_ref[row:row + SUBLANES, cols] = acc
        return carry

    lax.fori_loop(0, n_ch // LANES, lane_chunk, 0)

    y = _layer_norm_rows(y_ref[...]) * g_ref[...] + b_ref[...]
    o_ref[...] = (y * _sigmoid(y)).astype(o_ref.dtype)


def _conv_call(u, dw, dw_b, ln_g, ln_b, rows, seq, ctx_len, t_lat, tile):
    width, n_ch = dw.shape
    assert width // 2 < CONV_HALO and tile % CONV_HALO == 0
    hb = tile // CONV_HALO
    n_halo_blocks = rows // CONV_HALO
    dw_pad = jnp.zeros((2 * CONV_HALO, n_ch), F32).at[:width].set(dw)
    vec = pl.BlockSpec((1, n_ch), lambda i: (0, 0))
    sh_rows = tile + (CONV_HALO - width // 2 + width - 1) // SUBLANES * SUBLANES
    assert SUBLANES - 1 + sh_rows <= tile + 2 * CONV_HALO
    return pl.pallas_call(
        functools.partial(_conv_kernel, rows_tile=tile, width=width, seq=seq, ctx_len=ctx_len,
                          t_lat=t_lat),
        grid=(rows // tile,),
        in_specs=[pl.BlockSpec((CONV_HALO, n_ch), lambda i: (jnp.maximum(i * hb - 1, 0), 0)),
                  pl.BlockSpec((tile, n_ch), lambda i: (i, 0)),
                  pl.BlockSpec((CONV_HALO, n_ch),
                               lambda i: (jnp.minimum((i + 1) * hb, n_halo_blocks - 1), 0)),
                  pl.BlockSpec((2 * CONV_HALO, n_ch), lambda i: (0, 0)), vec, vec, vec],
        out_specs=pl.BlockSpec((tile, n_ch), lambda i: (i, 0)),
        out_shape=jax.ShapeDtypeStruct((rows, n_ch), BF16),
        scratch_shapes=[pltpu.VMEM((tile + 2 * CONV_HALO, n_ch), F32),
                        pltpu.VMEM((SUBLANES - 1, sh_rows, n_ch), F32),
                        pltpu.VMEM((tile, n_ch), F32)],
        compiler_params=_params("parallel"),
        name="conformer_conv",
    )(u, u, u, dw_pad, dw_b.reshape(1, n_ch), ln_g.reshape(1, n_ch), ln_b.reshape(1, n_ch))


def kernel(x, c, ctx, c_ctx, w_mod, b_mod, ln_g, ln_b, ffn1_w_gate, ffn1_w_up, ffn1_w_down,
           w_in, b_glu, conv_dw, conv_dw_b, conv_ln_g, conv_ln_b, lambda_q1, lambda_k1,
           lambda_q2, lambda_k2, attn_subln_g, w_out, ffn2_w_gate, ffn2_w_up, ffn2_w_down):
    batch, seq, d = x.shape
    ctx_len = ctx.shape[1]
    depth = w_mod.shape[0]
    d_ff = ffn1_w_gate.shape[2]
    conv_ch = conv_dw.shape[2]
    d_head = lambda_q1.shape[1]
    dv = attn_subln_g.shape[1]
    attn_w = w_out.shape[1] - conv_ch
    n_heads = attn_w // dv
    t_lat, t_ctx = batch * seq, batch * ctx_len
    t_all = t_lat + t_ctx
    alpha = (2 * depth) ** 0.25
    assert batch + 1 <= MOD_ROWS and dv == 2 * d_head and t_lat % ctx_len == 0

    t = _tile_plan(seq, ctx_len, t_ctx, d, d_ff, conv_ch, attn_w)

    c_rows = jnp.zeros((MOD_ROWS, d), F32).at[:batch].set(c).at[batch].set(c_ctx)
    mods = _mod_call(c_rows, w_mod, b_mod).reshape(depth, MOD_ROWS * N_MOD, 1, d)

    xr, xm = _entry_call(x.reshape(t_lat, d), ctx.reshape(t_ctx, d), mods[0], seq, batch, t.norm_rows)

    tab = _rope_tables(seq, t.rows, d_head)

    for i in range(depth):
        last = i == depth - 1
        lam_init = 0.8 - 0.6 * math.exp(-0.3 * i)
        mod = mods[i]
        norm = functools.partial(_postnorm_call, seq=seq, batch=batch, alpha=alpha, tm=t.norm_rows)

        hid = _dual_call(xm, ffn1_w_gate, 0, ffn1_w_up, 0, i, d_ff, None, "swiglu", BF16, t_all,
                         t.rows, t.ffn_cols)
        y = _matmul_call([hid], ffn1_w_down, i, d, BF16, t_all, t.rows, t.down_cols)
        xr, xm = norm(xr, y, mod, 2, 0.5, ln_g[i, 0], ln_b[i, 0], mod, (3, 4), t_all)

        rows = t_lat if last else t_all
        u = _dual_call(xm, w_in, 0, w_in, conv_ch, i, conv_ch, b_glu[i].reshape(1, -1), "glu",
                       F32, rows, t.rows, t.glu_cols)
        head_proj = functools.partial(_head_proj_call, xm, w_in, i, n_out=attn_w, seq=seq,
                                      t_lat=t_lat, d_head=d_head, tm=t.rows, tn=t.head_cols)
        qt = head_proj(col_off=2 * conv_ch, tab=tab, scale=d_head ** -0.5 * LOG2_E,
                       transpose=True, rows=rows)
        kk = head_proj(col_off=2 * conv_ch + attn_w, tab=tab, scale=1.0, transpose=False,
                       rows=t_all)
        vt = head_proj(col_off=2 * conv_ch + 2 * attn_w, tab=None, scale=1.0, transpose=True,
                       rows=t_all)
        lam_vecs = jnp.zeros((4, LANES), F32).at[:, :d_head].set(
            jnp.stack([lambda_q1[i], lambda_k1[i], lambda_q2[i], lambda_k2[i]]).astype(F32))
        attn, attn_ctx = _attn_call(qt, kk, vt, lam_vecs, attn_subln_g[i], lam_init, batch, seq,
                                    ctx_len, n_heads, d_head, not last, t.queries)
        if not last:
            attn = jnp.concatenate([attn, attn_ctx], axis=0)
        conv = _conv_call(u, conv_dw[i], conv_dw_b[i], conv_ln_g[i], conv_ln_b[i], rows, seq,
                          ctx_len, t_lat, t.conv_rows)
        y = _matmul_call([conv, attn], w_out, i, d, BF16, rows, t.rows, t.out_cols)
        xr, xm = norm(xr, y, mod, 5, 1.0, ln_g[i, 1], ln_b[i, 1], mod, (6, 7), rows)

        hid = _dual_call(xm, ffn2_w_gate, 0, ffn2_w_up, 0, i, d_ff, None, "swiglu", BF16, rows,
                         t.rows, t.ffn_cols)
        y = _matmul_call([hid], ffn2_w_down, i, d, BF16, rows, t.rows, t.down_cols)
        nxt = None if last else mods[i + 1]
        xr, xm = norm(xr, y, mod, 8, 0.5, ln_g[i, 2], ln_b[i, 2], nxt, (0, 1), rows)

    return xr.reshape(batch, seq, d)
```

```python
import functools
import math
from typing import NamedTuple

import jax
import jax.numpy as jnp
import numpy as np
from jax import lax
from jax.experimental import pallas as pl
from jax.experimental.pallas import tpu as pltpu

GRID_W = 64
ROPE_BASE = 10000.0
LN_EPS = 1e-5
N_MOD = 9
LANES = 128
SUBLANES = 8
BF16_ROWS = 16
MOD_ROWS = 8
ATTN_KEY_CHUNK = 2048
ATTN_CHAINS = 8
LOG2_E = 1.4426950408889634
CONV_HALO = 16
VMEM_LIMIT_BYTES = 56 * 1024 * 1024

BF16 = jnp.bfloat16
F32 = jnp.float32


def _pick(n, pref):
    t = min(n, pref)
    while n % t:
        t //= 2
    return t


class _Tiles(NamedTuple):
    rows: int
    norm_rows: int
    conv_rows: int
    queries: int
    ffn_cols: int
    glu_cols: int
    down_cols: int
    out_cols: int
    head_cols: int


def _tile_plan(seq, ctx_len, t_ctx, d, d_ff, conv_ch, attn_w):
    unit = math.gcd(seq, t_ctx)
    return _Tiles(rows=_pick(unit, 1024), norm_rows=_pick(unit, 256),
                  conv_rows=_pick(math.gcd(seq, ctx_len), 256), queries=_pick(seq, 1024),
                  ffn_cols=_pick(d_ff, 512), glu_cols=_pick(conv_ch, 512),
                  down_cols=_pick(d, 512), out_cols=_pick(d, 1024),
                  head_cols=_pick(attn_w, 1024))


def _params(*sem):
    return pltpu.CompilerParams(dimension_semantics=sem, vmem_limit_bytes=VMEM_LIMIT_BYTES)


def _sigmoid(v):
    return 1.0 / (1.0 + jnp.exp(-v))


def _group_of(row0, seq, batch):
    return jnp.minimum(row0 // seq, batch)


def _mod_spec(d, tm, seq, batch, slot):
    return pl.BlockSpec((1, 1, d), lambda i: (_group_of(i * tm, seq, batch) * N_MOD + slot, 0, 0))


def _mod_kernel(c_ref, w_ref, b_ref, o_ref):
    c = c_ref[...]
    s = (c * _sigmoid(c)).astype(BF16)
    w = w_ref[0].astype(BF16)
    o_ref[0] = jnp.dot(s, w, preferred_element_type=F32) + b_ref[0]


def _mod_call(c_rows, w_mod, b_mod):
    depth, d, n = w_mod.shape
    tn = _pick(n, 512)
    return pl.pallas_call(
        _mod_kernel,
        grid=(depth, n // tn),
        in_specs=[pl.BlockSpec((MOD_ROWS, d), lambda l, j: (0, 0)),
                  pl.BlockSpec((1, d, tn), lambda l, j: (l, 0, j)),
                  pl.BlockSpec((1, 1, tn), lambda l, j: (l, 0, j))],
        out_specs=pl.BlockSpec((1, MOD_ROWS, tn), lambda l, j: (l, 0, j)),
        out_shape=jax.ShapeDtypeStruct((depth, MOD_ROWS, n), F32),
        compiler_params=_params("parallel", "parallel"),
        name="adaln_mod",
    )(c_rows, w_mod, b_mod.reshape(depth, 1, n))


def _layer_norm_rows(z):
    mu = jnp.mean(z, axis=-1, keepdims=True)
    zc = z - mu
    var = jnp.mean(zc * zc, axis=-1, keepdims=True)
    return zc * lax.rsqrt(var + LN_EPS)


def _entry_kernel(x_ref, c_ref, sh_ref, sc_ref, xn_ref, xm_ref, *, n_lat_tiles):
    def emit(src_ref):
        y = _layer_norm_rows(src_ref[...])
        xn_ref[...] = y
        xm_ref[...] = (y * (1.0 + sc_ref[0]) + sh_ref[0]).astype(BF16)

    i = pl.program_id(0)
    pl.when(i < n_lat_tiles)(lambda: emit(x_ref))
    pl.when(i >= n_lat_tiles)(lambda: emit(c_ref))


def _entry_call(x2, ctx2, mod, seq, batch, tm):
    (t_lat, d), t_ctx = x2.shape, ctx2.shape[0]
    n_lat = t_lat // tm
    t = t_lat + t_ctx
    row = pl.BlockSpec((tm, d), lambda i: (i, 0))
    return pl.pallas_call(
        functools.partial(_entry_kernel, n_lat_tiles=n_lat),
        grid=(t // tm,),
        in_specs=[pl.BlockSpec((tm, d), lambda i: (jnp.minimum(i, n_lat - 1), 0)),
                  pl.BlockSpec((tm, d), lambda i: (jnp.maximum(i - n_lat, 0), 0)),
                  _mod_spec(d, tm, seq, batch, 0), _mod_spec(d, tm, seq, batch, 1)],
        out_specs=[row, row],
        out_shape=[jax.ShapeDtypeStruct((t, d), F32), jax.ShapeDtypeStruct((t, d), BF16)],
        compiler_params=_params("parallel"),
        name="entry_norm",
    )(x2, ctx2, mod, mod)


def _postnorm_kernel(*refs, alpha, coef, with_next):
    if with_next:
        x_ref, y_ref, gate_ref, g_ref, b_ref, sh_ref, sc_ref, xn_ref, xm_ref = refs
    else:
        x_ref, y_ref, gate_ref, g_ref, b_ref, xn_ref = refs
    z = alpha * x_ref[...] + (coef * gate_ref[0]) * y_ref[...]
    y = _layer_norm_rows(z) * g_ref[...] + b_ref[...]
    xn_ref[...] = y
    if with_next:
        xm_ref[...] = (y * (1.0 + sc_ref[0]) + sh_ref[0]).astype(BF16)


def _postnorm_call(x, y, mod_gate, gate_slot, coef, g, b, mod_next, next_slots,
                   rows, seq, batch, alpha, tm):
    d = x.shape[1]
    row = pl.BlockSpec((tm, d), lambda i: (i, 0))
    deep = pl.BlockSpec((tm, d), lambda i: (i, 0), pipeline_mode=pl.Buffered(3))
    vec = pl.BlockSpec((1, d), lambda i: (0, 0))
    with_next = mod_next is not None
    in_specs = [deep, deep, _mod_spec(d, tm, seq, batch, gate_slot), vec, vec]
    args = [x, y, mod_gate, g.reshape(1, d), b.reshape(1, d)]
    out_specs = [row]
    out_shape = [jax.ShapeDtypeStruct((rows, d), F32)]
    if with_next:
        in_specs += [_mod_spec(d, tm, seq, batch, next_slots[0]),
                     _mod_spec(d, tm, seq, batch, next_slots[1])]
        args += [mod_next, mod_next]
        out_specs.append(row)
        out_shape.append(jax.ShapeDtypeStruct((rows, d), BF16))
    body = functools.partial(_postnorm_kernel, alpha=alpha, coef=coef, with_next=with_next)

    def outer(*refs):
        pltpu.emit_pipeline(body, grid=(rows // tm,), in_specs=in_specs, out_specs=out_specs)(*refs)

    any_spec = pl.BlockSpec(memory_space=pl.ANY)
    out = pl.pallas_call(
        outer,
        in_specs=[any_spec] * len(args),
        out_specs=[any_spec] * len(out_shape),
        out_shape=out_shape,
        compiler_params=pltpu.CompilerParams(vmem_limit_bytes=VMEM_LIMIT_BYTES),
        name="post_norm",
    )(*args)
    return (out[0], out[1]) if with_next else (out[0], None)


def _stream_chunks(k, n_row_tiles):
    n = 1
    while n * 2 <= n_row_tiles and k % (n * 2 * BF16_ROWS) == 0:
        n *= 2
    return n


def _streamed_weight_spec(k, tn, n_chunks, n_sweeps, layer, row_off, col_off):
    kb = k // n_chunks
    assert row_off % kb == 0 and col_off % tn == 0

    def idx(s, i):
        return (layer, row_off // kb + jnp.minimum(i, n_chunks - 1),
                col_off // tn + jnp.minimum(s, n_sweeps - 1))
    return pl.BlockSpec((None, kb, tn), idx)


def _stage_weights(w_refs, wb_refs, n_chunks, n_sweeps):
    s, i = pl.program_id(0), pl.program_id(1)

    @pl.when(jnp.logical_and(i < n_chunks, s < n_sweeps))
    def _():
        for w_ref, wb_ref in zip(w_refs, wb_refs):
            kb = w_ref.shape[0]
            wb_ref[s % 2, pl.ds(pl.multiple_of(i * kb, kb), kb), :] = w_ref[...].astype(BF16)


def _row_tile(s, i):
    return jnp.where(s == 0, 0, i)


def _col_block(s):
    return jnp.maximum(s - 1, 0)


def _dual_kernel(*refs, act, with_bias, n_chunks, n_sweeps):
    if with_bias:
        a_ref, w1_ref, w2_ref, b1_ref, b2_ref, o_ref, w1b_ref, w2b_ref = refs
    else:
        a_ref, w1_ref, w2_ref, o_ref, w1b_ref, w2b_ref = refs
    _stage_weights([w1_ref, w2_ref], [w1b_ref, w2b_ref], n_chunks, n_sweeps)
    sweep = pl.program_id(0)
    slot = (sweep + 1) % 2

    @pl.when(sweep > 0)
    def _():
        a = a_ref[...]
        p1 = jnp.dot(a, w1b_ref[slot], preferred_element_type=F32)
        p2 = jnp.dot(a, w2b_ref[slot], preferred_element_type=F32)
        if with_bias:
            p1 = p1 + b1_ref[...]
            p2 = p2 + b2_ref[...]
        if act == "swiglu":
            r = (p1 * _sigmoid(p1)) * p2
        else:
            r = p1 * _sigmoid(p2)
        o_ref[...] = r.astype(o_ref.dtype)


def _dual_call(a, w1, off1, w2, off2, layer, n_out, bias, act, out_dtype, rows, tm, tn):
    k = a.shape[1]
    with_bias = bias is not None
    n_sweeps, n_tiles = n_out // tn, rows // tm
    n_chunks = _stream_chunks(k, n_tiles)
    in_specs = [pl.BlockSpec((tm, k), lambda s, i: (_row_tile(s, i), 0)),
                _streamed_weight_spec(k, tn, n_chunks, n_sweeps, layer, 0, off1),
                _streamed_weight_spec(k, tn, n_chunks, n_sweeps, layer, 0, off2)]
    args = [a, w1, w2]
    if with_bias:
        in_specs += [pl.BlockSpec((1, tn), lambda s, i: (0, _col_block(s) + off1 // tn)),
                     pl.BlockSpec((1, tn), lambda s, i: (0, _col_block(s) + off2 // tn))]
        args += [bias, bias]
    return pl.pallas_call(
        functools.partial(_dual_kernel, act=act, with_bias=with_bias, n_chunks=n_chunks,
                          n_sweeps=n_sweeps),
        grid=(n_sweeps + 1, n_tiles),
        in_specs=in_specs,
        out_specs=pl.BlockSpec((tm, tn), lambda s, i: (_row_tile(s, i), _col_block(s))),
        out_shape=jax.ShapeDtypeStruct((rows, n_out), out_dtype),
        scratch_shapes=[pltpu.VMEM((2, k, tn), BF16), pltpu.VMEM((2, k, tn), BF16)],
        compiler_params=_params("arbitrary", "arbitrary"),
        name="gated_proj_" + act,
    )(*args)


def _matmul_kernel(*refs, n_pairs, n_chunks, n_sweeps):
    a_refs, w_refs = refs[:n_pairs], refs[n_pairs:2 * n_pairs]
    o_ref, wb_refs = refs[2 * n_pairs], refs[2 * n_pairs + 1:]
    _stage_weights(w_refs, wb_refs, n_chunks, n_sweeps)
    sweep = pl.program_id(0)
    slot = (sweep + 1) % 2

    @pl.when(sweep > 0)
    def _():
        acc = jnp.dot(a_refs[0][...], wb_refs[0][slot], preferred_element_type=F32)
        for a_ref, wb_ref in zip(a_refs[1:], wb_refs[1:]):
            acc = acc + jnp.dot(a_ref[...], wb_ref[slot], preferred_element_type=F32)
        o_ref[...] = acc.astype(o_ref.dtype)


def _matmul_call(a_list, w, layer, n_out, out_dtype, rows, tm, tn):
    n_pairs = len(a_list)
    n_sweeps, n_tiles = n_out // tn, rows // tm
    n_chunks = min(_stream_chunks(a.shape[1], n_tiles) for a in a_list)
    in_specs, w_specs, scratch, row_off = [], [], [], 0
    for a in a_list:
        kp = a.shape[1]
        in_specs.append(pl.BlockSpec((tm, kp), lambda s, i: (_row_tile(s, i), 0)))
        w_specs.append(_streamed_weight_spec(kp, tn, n_chunks, n_sweeps, layer, row_off, 0))
        scratch.append(pltpu.VMEM((2, kp, tn), BF16))
        row_off += kp
    return pl.pallas_call(
        functools.partial(_matmul_kernel, n_pairs=n_pairs, n_chunks=n_chunks, n_sweeps=n_sweeps),
        grid=(n_sweeps + 1, n_tiles),
        in_specs=in_specs + w_specs,
        out_specs=pl.BlockSpec((tm, tn), lambda s, i: (_row_tile(s, i), _col_block(s))),
        out_shape=jax.ShapeDtypeStruct((rows, n_out), out_dtype),
        scratch_shapes=scratch,
        compiler_params=_params("arbitrary", "arbitrary"),
        name="proj",
    )(*a_list, *([w] * n_pairs))


def _rope_tables(seq, tm, d_head):
    axis_dim = d_head // 2
    half = axis_dim // 2
    inv = (1.0 / (ROPE_BASE ** (np.arange(half, dtype=np.float32) / half))).astype(np.float32)
    pos = np.arange(seq)
    r = (pos // GRID_W).astype(np.float32)
    col = (pos % GRID_W).astype(np.float32)
    ang_r = r[:, None] * inv[None, :]
    ang_c = col[:, None] * inv[None, :]
    ang = np.concatenate([ang_r, ang_r, ang_c, ang_c], axis=-1)
    ang = np.tile(ang, (1, LANES // d_head))
    first_half = (np.arange(LANES) % axis_dim) < half
    cos, sin = np.cos(ang), np.sin(ang)
    sin_up = np.where(first_half[None, :], -sin, 0.0)
    sin_dn = np.where(first_half[None, :], 0.0, sin)
    ident = np.stack([np.ones((tm, LANES)), np.zeros((tm, LANES)), np.zeros((tm, LANES))])
    tab = np.concatenate([np.stack([cos, sin_up, sin_dn]), ident], axis=1)
    return jnp.asarray(tab.astype(np.float32))


def _head_proj_kernel(*refs, rope, half, scale, transpose, n_chunks, n_sweeps):
    if rope:
        a_ref, w_ref, tab_ref, o_ref, wb_ref = refs
    else:
        a_ref, w_ref, o_ref, wb_ref = refs
    _stage_weights([w_ref], [wb_ref], n_chunks, n_sweeps)
    sweep = pl.program_id(0)
    slot = (sweep + 1) % 2

    @pl.when(sweep > 0)
    def _():
        acc = jnp.dot(a_ref[...], wb_ref[slot], preferred_element_type=F32)
        for c in range(acc.shape[1] // LANES):
            cols = slice(c * LANES, (c + 1) * LANES)
            r = acc[:, cols]
            if rope:
                up = pltpu.roll(r, LANES - half, 1)
                dn = pltpu.roll(r, half, 1)
                r = r * tab_ref[0] + up * tab_ref[1] + dn * tab_ref[2]
            if scale != 1.0:
                r = r * scale
            if transpose:
                o_ref[cols, :] = r.T.astype(o_ref.dtype)
            else:
                o_ref[:, cols] = r.astype(o_ref.dtype)


def _head_proj_call(a, w, layer, col_off, n_out, tab, seq, t_lat, d_head, scale, transpose, rows,
                    tm, tn):
    k = a.shape[1]
    rope = tab is not None
    n_sweeps, n_tiles = n_out // tn, rows // tm
    n_chunks = _stream_chunks(k, n_tiles)

    def tab_idx(s, i):
        r0 = _row_tile(s, i) * tm
        return (0, jnp.where(r0 < t_lat, (r0 % seq) // tm, seq // tm), 0)

    in_specs = [pl.BlockSpec((tm, k), lambda s, i: (_row_tile(s, i), 0)),
                _streamed_weight_spec(k, tn, n_chunks, n_sweeps, layer, 0, col_off)]
    args = [a, w]
    if rope:
        in_specs.append(pl.BlockSpec((3, tm, LANES), tab_idx))
        args.append(tab)
    if transpose:
        out_spec = pl.BlockSpec((tn, tm), lambda s, i: (_col_block(s), _row_tile(s, i)))
        out_shape = jax.ShapeDtypeStruct((n_out, rows), BF16)
    else:
        out_spec = pl.BlockSpec((tm, tn), lambda s, i: (_row_tile(s, i), _col_block(s)))
        out_shape = jax.ShapeDtypeStruct((rows, n_out), BF16)
    return pl.pallas_call(
        functools.partial(_head_proj_kernel, rope=rope, half=d_head // 4, scale=scale,
                          transpose=transpose, n_chunks=n_chunks, n_sweeps=n_sweeps),
        grid=(n_sweeps + 1, n_tiles),
        in_specs=in_specs,
        out_specs=out_spec,
        out_shape=out_shape,
        scratch_shapes=[pltpu.VMEM((2, k, tn), BF16)],
        compiler_params=_params("arbitrary", "arbitrary"),
        name="head_proj",
    )(*args)


NEG_BIG = -1e30
BOUND_MARGIN = 1.01
BOUND_OK = 2.0 ** -80


def _max_key_sqnorm(k_refs, d_head):
    d = lax.broadcasted_iota(jnp.int32, (2 * d_head, LANES), 0)
    m = lax.broadcasted_iota(jnp.int32, (2 * d_head, LANES), 1)
    pick = jnp.where(d // d_head == m, 1.0, 0.0).astype(BF16)
    best = jnp.zeros((1, LANES), F32)
    for k_ref in k_refs:
        k = k_ref[...]
        sq = jnp.dot(k * k, pick, preferred_element_type=F32)
        best = jnp.maximum(best, jnp.max(sq, axis=0, keepdims=True))
    return best


def _diff_attend(qts, segments, lam, gain, d_head, key_chunk, key_sqnorm=None):
    bounded = key_sqnorm is not None
    chunks = [(k_ref, vt_ref, slice(c0, min(c0 + key_chunk, k_ref.shape[0])))
              for k_ref, vt_ref in segments for c0 in range(0, k_ref.shape[0], key_chunk)]

    def scores(qmt, chunk):
        return jnp.dot(chunk[0][chunk[2], :], qmt, preferred_element_type=F32)

    state = []
    for qt in qts:
        n = qt.shape[1]
        row = lax.broadcasted_iota(jnp.int32, qt.shape, 0)
        zero = jnp.zeros_like(qt)
        qmt = jnp.concatenate([jnp.where(row < d_head, qt, zero),
                               jnp.where(row >= d_head, qt, zero)], axis=1)
        if bounded:
            q2 = qmt.astype(F32)
            q2 = jnp.sum(q2 * q2, axis=0, keepdims=True)
            k2 = jnp.concatenate([jnp.broadcast_to(key_sqnorm[:, 0:1], (1, n)),
                                  jnp.broadcast_to(key_sqnorm[:, 1:2], (1, n))], axis=1)
            m = jnp.sqrt(q2 * k2) * BOUND_MARGIN
        else:
            m = jnp.full((1, 2 * n), NEG_BIG, F32)
        state.append(dict(qmt=qmt, m=m, l=jnp.zeros((1, 2 * n), F32),
                          acc=jnp.zeros((qt.shape[0], 2 * n), F32),
                          s_next=scores(qmt, chunks[0])))
    for i, (k_ref, vt_ref, keys) in enumerate(chunks):
        for st in state:
            s = st["s_next"]
            st["s_next"] = scores(st["qmt"], chunks[i + 1]) if i + 1 < len(chunks) else None
            if bounded:
                p = jnp.exp2(s - st["m"])
                st["l"] = st["l"] + jnp.sum(p, axis=0, keepdims=True)
                st["acc"] = st["acc"] + jnp.dot(vt_ref[:, keys], p.astype(BF16),
                                                preferred_element_type=F32)
            else:
                m_new = jnp.maximum(st["m"], jnp.max(s, axis=0, keepdims=True))
                alpha = jnp.exp2(st["m"] - m_new)
                p = jnp.exp2(s - m_new)
                st["l"] = st["l"] * alpha + jnp.sum(p, axis=0, keepdims=True)
                st["acc"] = st["acc"] * alpha + jnp.dot(vt_ref[:, keys], p.astype(BF16),
                                                        preferred_element_type=F32)
                st["m"] = m_new
    outs = []
    for st in state:
        n = st["qmt"].shape[1] // 2
        o = st["acc"] * (1.0 / st["l"])
        d = (o[:, :n] - lam * o[:, n:]).T
        outs.append(d * lax.rsqrt(jnp.mean(d * d, axis=-1, keepdims=True) + LN_EPS) * gain)
    return outs, functools.reduce(jnp.minimum, [jnp.min(st["l"]) for st in state])


def _attn_kernel(*refs, lam_init, d_head, key_chunk, with_ctx_queries, chains):
    if with_ctx_queries:
        (lam_ref, g_ref, qt_ref, kc_ref, vtc_ref, kl_ref, vtl_ref, qtc_ref, o_ref, oc_ref,
         ksq_ref) = refs
    else:
        lam_ref, g_ref, qt_ref, kc_ref, vtc_ref, kl_ref, vtl_ref, o_ref, ksq_ref = refs
    lv = lam_ref[...]
    lam = (jnp.exp(jnp.sum(lv[0:1] * lv[1:2], axis=-1, keepdims=True))
           - jnp.exp(jnp.sum(lv[2:3] * lv[3:4], axis=-1, keepdims=True)) + lam_init)
    gain = g_ref[...] * (1.0 - lam_init)
    first_step = pl.program_id(2) == 0

    @pl.when(first_step)
    def _():
        ksq_ref[...] = _max_key_sqnorm([kc_ref, kl_ref], d_head)

    def attend(q_groups, segments, write):
        outs, l_min = _diff_attend(q_groups, segments, lam, gain, d_head, key_chunk, ksq_ref[...])
        write(outs)

        @pl.when(jnp.logical_not(l_min >= BOUND_OK))
        def _():
            write(_diff_attend(q_groups, segments, lam, gain, d_head, key_chunk)[0])

    n = qt_ref.shape[1] // chains
    groups = [slice(g * n, (g + 1) * n) for g in range(chains)]

    def write_latent(outs):
        for g, o in zip(groups, outs):
            o_ref[g, :] = o.astype(o_ref.dtype)

    attend([qt_ref[:, g] for g in groups], [(kc_ref, vtc_ref), (kl_ref, vtl_ref)], write_latent)

    if with_ctx_queries:
        def write_ctx(outs):
            oc_ref[...] = outs[0].astype(oc_ref.dtype)

        @pl.when(first_step)
        def _():
            attend([qtc_ref[...]], [(kc_ref, vtc_ref)], write_ctx)


def _attn_call(qt, kk, vt, lam_vecs, subln_g, lam_init, batch, seq, ctx_len, n_heads, d_head,
               with_ctx_queries, tq, key_chunk=ATTN_KEY_CHUNK, chains=None):
    dv = 2 * d_head
    assert dv == LANES
    t_lat = batch * seq
    n_lat = seq // tq
    ctx_blk = t_lat // ctx_len
    if chains is None:
        chains = min(ATTN_CHAINS, tq // LANES)

    full = lambda shape: pl.BlockSpec(shape, lambda b, h, qi: (0, 0))
    in_specs = [full((4, LANES)), full((1, dv)),
                pl.BlockSpec((dv, tq), lambda b, h, qi: (h, b * n_lat + qi)),
                pl.BlockSpec((ctx_len, dv), lambda b, h, qi: (ctx_blk + b, h)),
                pl.BlockSpec((dv, ctx_len), lambda b, h, qi: (h, ctx_blk + b)),
                pl.BlockSpec((seq, dv), lambda b, h, qi: (b, h)),
                pl.BlockSpec((dv, seq), lambda b, h, qi: (h, b))]
    args = [lam_vecs, subln_g.reshape(1, dv), qt, kk, vt, kk, vt]
    out_specs = [pl.BlockSpec((tq, dv), lambda b, h, qi: (b * n_lat + qi, h))]
    out_shape = [jax.ShapeDtypeStruct((t_lat, n_heads * dv), BF16)]
    if with_ctx_queries:
        in_specs.append(pl.BlockSpec((dv, ctx_len), lambda b, h, qi: (h, ctx_blk + b)))
        args.append(qt)
        out_specs.append(pl.BlockSpec((ctx_len, dv), lambda b, h, qi: (b, h)))
        out_shape.append(jax.ShapeDtypeStruct((batch * ctx_len, n_heads * dv), BF16))
    out = pl.pallas_call(
        functools.partial(_attn_kernel, lam_init=lam_init, d_head=d_head, key_chunk=key_chunk,
                          with_ctx_queries=with_ctx_queries, chains=chains),
        grid=(batch, n_heads, n_lat),
        in_specs=in_specs,
        out_specs=out_specs,
        out_shape=out_shape,
        scratch_shapes=[pltpu.VMEM((1, LANES), F32)],
        compiler_params=_params("parallel", "parallel", "arbitrary"),
        name="diff_attention",
    )(*args)
    return (out[0], out[1]) if with_ctx_queries else (out[0], None)


def _conv_kernel(prev_ref, cur_ref, next_ref, dw_ref, dwb_ref, g_ref, b_ref, o_ref,
                 xs_ref, sh_ref, y_ref, *, rows_tile, width, seq, ctx_len, t_lat):
    i = pl.program_id(0)
    r0 = i * rows_tile
    is_lat = r0 < t_lat
    per_seq = jnp.where(is_lat, seq // rows_tile, ctx_len // rows_tile)
    pos = jnp.where(is_lat, i % (seq // rows_tile), (i - t_lat // rows_tile) % (ctx_len // rows_tile))
    keep_prev = (pos > 0).astype(F32)
    keep_next = (pos < per_seq - 1).astype(F32)
    h = CONV_HALO
    xs_ref[0:h, :] = jnp.where(keep_prev > 0, prev_ref[...], 0.0)
    xs_ref[h:h + rows_tile, :] = cur_ref[...]
    xs_ref[h + rows_tile:2 * h + rows_tile, :] = jnp.where(keep_next > 0, next_ref[...], 0.0)

    base = h - width // 2
    n_ch = cur_ref.shape[1]
    sh_rows = sh_ref.shape[1]
    for s in range(1, SUBLANES):
        sh_ref[s - 1] = xs_ref[s:s + sh_rows, :]

    def lane_chunk(c, carry):
        cols = pl.ds(pl.multiple_of(c * LANES, LANES), LANES)
        taps = [jnp.broadcast_to(dw_ref[t:t + 1, cols], (SUBLANES, LANES)) for t in range(width)]
        bias = jnp.broadcast_to(dwb_ref[:, cols], (SUBLANES, LANES))
        for row in range(0, rows_tile, SUBLANES):
            acc = bias
            for t in range(width):
                s, r0 = (base + t) % SUBLANES, row + (base + t) // SUBLANES * SUBLANES
                src = xs_ref[r0:r0 + SUBLANES, cols] if s == 0 else sh_ref[s - 1, r0:r0 + SUBLANES, cols]
                acc = acc + src * taps[t]
            y_ref[row:row + SUBLANES, cols] = acc
        return carry

    lax.fori_loop(0, n_ch // LANES, lane_chunk, 0)

    y = _layer_norm_rows(y_ref[...]) * g_ref[...] + b_ref[...]
    o_ref[...] = (y * _sigmoid(y)).astype(o_ref.dtype)


def _conv_call(u, dw, dw_b, ln_g, ln_b, rows, seq, ctx_len, t_lat, tile):
    width, n_ch = dw.shape
    assert width // 2 < CONV_HALO and tile % CONV_HALO == 0
    hb = tile // CONV_HALO
    n_halo_blocks = rows // CONV_HALO
    dw_pad = jnp.zeros((2 * CONV_HALO, n_ch), F32).at[:width].set(dw)
    vec = pl.BlockSpec((1, n_ch), lambda i: (0, 0))
    sh_rows = tile + (CONV_HALO - width // 2 + width - 1) // SUBLANES * SUBLANES
    assert SUBLANES - 1 + sh_rows <= tile + 2 * CONV_HALO
    return pl.pallas_call(
        functools.partial(_conv_kernel, rows_tile=tile, width=width, seq=seq, ctx_len=ctx_len,
                          t_lat=t_lat),
        grid=(rows // tile,),
        in_specs=[pl.BlockSpec((CONV_HALO, n_ch), lambda i: (jnp.maximum(i * hb - 1, 0), 0)),
                  pl.BlockSpec((tile, n_ch), lambda i: (i, 0)),
                  pl.BlockSpec((CONV_HALO, n_ch),
                               lambda i: (jnp.minimum((i + 1) * hb, n_halo_blocks - 1), 0)),
                  pl.BlockSpec((2 * CONV_HALO, n_ch), lambda i: (0, 0)), vec, vec, vec],
        out_specs=pl.BlockSpec((tile, n_ch), lambda i: (i, 0)),
        out_shape=jax.ShapeDtypeStruct((rows, n_ch), BF16),
        scratch_shapes=[pltpu.VMEM((tile + 2 * CONV_HALO, n_ch), F32),
                        pltpu.VMEM((SUBLANES - 1, sh_rows, n_ch), F32),
                        pltpu.VMEM((tile, n_ch), F32)],
        compiler_params=_params("parallel"),
        name="conformer_conv",
    )(u, u, u, dw_pad, dw_b.reshape(1, n_ch), ln_g.reshape(1, n_ch), ln_b.reshape(1, n_ch))


def kernel(x, c, ctx, c_ctx, w_mod, b_mod, ln_g, ln_b, ffn1_w_gate, ffn1_w_up, ffn1_w_down,
           w_in, b_glu, conv_dw, conv_dw_b, conv_ln_g, conv_ln_b, lambda_q1, lambda_k1,
           lambda_q2, lambda_k2, attn_subln_g, w_out, ffn2_w_gate, ffn2_w_up, ffn2_w_down):
    batch, seq, d = x.shape
    ctx_len = ctx.shape[1]
    depth = w_mod.shape[0]
    d_ff = ffn1_w_gate.shape[2]
    conv_ch = conv_dw.shape[2]
    d_head = lambda_q1.shape[1]
    dv = attn_subln_g.shape[1]
    attn_w = w_out.shape[1] - conv_ch
    n_heads = attn_w // dv
    t_lat, t_ctx = batch * seq, batch * ctx_len
    t_all = t_lat + t_ctx
    alpha = (2 * depth) ** 0.25
    assert batch + 1 <= MOD_ROWS and dv == 2 * d_head and t_lat % ctx_len == 0

    t = _tile_plan(seq, ctx_len, t_ctx, d, d_ff, conv_ch, attn_w)

    c_rows = jnp.zeros((MOD_ROWS, d), F32).at[:batch].set(c).at[batch].set(c_ctx)
    mods = _mod_call(c_rows, w_mod, b_mod).reshape(depth, MOD_ROWS * N_MOD, 1, d)

    xr, xm = _entry_call(x.reshape(t_lat, d), ctx.reshape(t_ctx, d), mods[0], seq, batch, t.norm_rows)

    tab = _rope_tables(seq, t.rows, d_head)

    for i in range(depth):
        last = i == depth - 1
        lam_init = 0.8 - 0.6 * math.exp(-0.3 * i)
        mod = mods[i]
        norm = functools.partial(_postnorm_call, seq=seq, batch=batch, alpha=alpha, tm=t.norm_rows)

        hid = _dual_call(xm, ffn1_w_gate, 0, ffn1_w_up, 0, i, d_ff, None, "swiglu", BF16, t_all,
                         t.rows, t.ffn_cols)
        y = _matmul_call([hid], ffn1_w_down, i, d, BF16, t_all, t.rows, t.down_cols)
        xr, xm = norm(xr, y, mod, 2, 0.5, ln_g[i, 0], ln_b[i, 0], mod, (3, 4), t_all)

        rows = t_lat if last else t_all
        u = _dual_call(xm, w_in, 0, w_in, conv_ch, i, conv_ch, b_glu[i].reshape(1, -1), "glu",
                       F32, rows, t.rows, t.glu_cols)
        head_proj = functools.partial(_head_proj_call, xm, w_in, i, n_out=attn_w, seq=seq,
                                      t_lat=t_lat, d_head=d_head, tm=t.rows, tn=t.head_cols)
        qt = head_proj(col_off=2 * conv_ch, tab=tab, scale=d_head ** -0.5 * LOG2_E,
                       transpose=True, rows=rows)
        kk = head_proj(col_off=2 * conv_ch + attn_w, tab=tab, scale=1.0, transpose=False,
                       rows=t_all)
        vt = head_proj(col_off=2 * conv_ch + 2 * attn_w, tab=None, scale=1.0, transpose=True,
                       rows=t_all)
        lam_vecs = jnp.zeros((4, LANES), F32).at[:, :d_head].set(
            jnp.stack([lambda_q1[i], lambda_k1[i], lambda_q2[i], lambda_k2[i]]).astype(F32))
        attn, attn_ctx = _attn_call(qt, kk, vt, lam_vecs, attn_subln_g[i], lam_init, batch, seq,
                                    ctx_len, n_heads, d_head, not last, t.queries)
        if not last:
            attn = jnp.concatenate([attn, attn_ctx], axis=0)
        conv = _conv_call(u, conv_dw[i], conv_dw_b[i], conv_ln_g[i], conv_ln_b[i], rows, seq,
                          ctx_len, t_lat, t.conv_rows)
        y = _matmul_call([conv, attn], w_out, i, d, BF16, rows, t.rows, t.out_cols)
        xr, xm = norm(xr, y, mod, 5, 1.0, ln_g[i, 1], ln_b[i, 1], mod, (6, 7), rows)

        hid = _dual_call(xm, ffn2_w_gate, 0, ffn2_w_up, 0, i, d_ff, None, "swiglu", BF16, rows,
                         t.rows, t.ffn_cols)
        y = _matmul_call([hid], ffn2_w_down, i, d, BF16, rows, t.rows, t.down_cols)
        nxt = None if last else mods[i + 1]
        xr, xm = norm(xr, y, mod, 8, 0.5, ln_g[i, 2], ln_b[i, 2], nxt, (0, 1), rows)

    return xr.reshape(batch, seq, d)
```
